```python
import math
import jax, jax.numpy as jnp
from jax import lax
import numpy as np

D_MODEL = 1024
BATCH = 2
SEQ = 8192
DEPTH = 4

MEM_LEN = 256
D_MIX = 2 * D_MODEL
D_S5 = D_MIX // 2
D_SSD = D_MIX - D_S5
S5_GROUP = 16
S5_GROUPS = D_S5 // S5_GROUP
S5_STATE = 64
SSD_HEADDIM = 64
SSD_HEADS = D_SSD // SSD_HEADDIM
SSD_GROUPS = 4
SSD_HPG = SSD_HEADS // SSD_GROUPS
SSD_STATE = 128
SSD_CONV = 4
SSD_CHUNK = 128
D_CONV_CH = D_SSD + 2 * SSD_GROUPS * SSD_STATE
D_IN_PROJ = D_S5 + D_SSD + D_CONV_CH + SSD_HEADS
XA_HEADS = 4
XA_HEAD_DIM = D_MODEL // XA_HEADS
D_FF = 4 * D_MODEL
EPS = 1e-5

kernel_name = "hybrid_s5_ssd_xattn_trunk"


def rms_norm(x, g):
    xf = x.astype(jnp.float32)
    y = xf * lax.rsqrt(jnp.mean(xf * xf, axis=-1, keepdims=True) + EPS)
    return (y * g.astype(jnp.float32)).astype(x.dtype)


def s5_mixer(u, a_re, a_im, log_dt, b_re, b_im, c_re, c_im, d, w_glu):
    bsz, seq, _ = u.shape
    f32 = jnp.float32
    uf = u.astype(f32).reshape(bsz, seq, S5_GROUPS, S5_GROUP)
    ar, ai = a_re.astype(f32), a_im.astype(f32)
    dt = jnp.exp(log_dt.astype(f32))[:, None]
    mag = jnp.exp(dt * ar)
    abar_r, abar_i = mag * jnp.cos(dt * ai), mag * jnp.sin(dt * ai)
    den = ar * ar + ai * ai
    zr, zi = abar_r - 1.0, abar_i
    fr = (zr * ar + zi * ai) / den
    fi = (zi * ar - zr * ai) / den
    br, bi = b_re.astype(f32), b_im.astype(f32)
    bbar_r = fr[..., None] * br - fi[..., None] * bi
    bbar_i = fr[..., None] * bi + fi[..., None] * br
    drive_r = jnp.einsum('bsgh,gph->bsgp', uf, bbar_r)
    drive_i = jnp.einsum('bsgh,gph->bsgp', uf, bbar_i)
    a_r = jnp.broadcast_to(abar_r, (1, seq) + abar_r.shape)
    a_i = jnp.broadcast_to(abar_i, (1, seq) + abar_i.shape)

    def combine(left, right):
        a1r, a1i, b1r, b1i = left
        a2r, a2i, b2r, b2i = right
        return (a2r * a1r - a2i * a1i,
                a2r * a1i + a2i * a1r,
                a2r * b1r - a2i * b1i + b2r,
                a2r * b1i + a2i * b1r + b2i)

    _, _, s_r, s_i = lax.associative_scan(combine, (a_r, a_i, drive_r, drive_i), axis=1)
    y = (jnp.einsum('bsgp,ghp->bsgh', s_r, c_re.astype(f32))
         - jnp.einsum('bsgp,ghp->bsgh', s_i, c_im.astype(f32))
         + d.astype(f32) * uf)
    y = jax.nn.gelu(y.reshape(bsz, seq, D_S5))
    y = y * jax.nn.sigmoid(y @ w_glu.astype(f32))
    return y.astype(u.dtype)


def segsum(a):
    t = a.shape[-1]
    aa = jnp.broadcast_to(a[..., :, None], a.shape + (t,))
    strict = jnp.tril(jnp.ones((t, t), dtype=bool), -1)
    cs = jnp.cumsum(jnp.where(strict, aa, 0.0), axis=-2)
    incl = jnp.tril(jnp.ones((t, t), dtype=bool), 0)
    return jnp.where(incl, cs, -jnp.inf)


def ssd_chunked(x, a, bm, cm):
    bsz, seq, g, r, p = x.shape
    n = bm.shape[-1]
    nc, L = seq // SSD_CHUNK, SSD_CHUNK
    x = x.reshape(bsz, nc, L, g, r, p)
    bm = bm.reshape(bsz, nc, L, g, n)
    cm = cm.reshape(bsz, nc, L, g, n)
    a = a.reshape(bsz, nc, L, g, r).transpose(0, 3, 4, 1, 2)
    a_cum = jnp.cumsum(a, axis=-1)
    lmat = jnp.exp(segsum(a))
    cb = jnp.einsum('bclgn,bcsgn->bcgls', cm, bm)
    y_diag = jnp.einsum('bcgls,bgrcls,bcsgrp->bclgrp', cb, lmat, x)
    decay_states = jnp.exp(a_cum[..., -1:] - a_cum)
    states = jnp.einsum('bcsgn,bgrcs,bcsgrp->bcgrpn', bm, decay_states, x)
    states = jnp.concatenate([jnp.zeros_like(states[:, :1]), states], axis=1)
    chunk_tot = jnp.pad(a_cum[..., -1], ((0, 0), (0, 0), (0, 0), (1, 0)))
    decay_chunk = jnp.exp(segsum(chunk_tot))
    new_states = jnp.einsum('bgrzc,bcgrpn->bzgrpn', decay_chunk, states)
    states_prev = new_states[:, :-1]
    y_off = jnp.einsum('bclgn,bcgrpn,bgrcl->bclgrp', cm, states_prev, jnp.exp(a_cum))
    return (y_diag + y_off).reshape(bsz, seq, g, r, p)


def ssd_mixer(z, xbc, dt_raw, conv_w, conv_b, dt_bias, a_log, d_skip, norm_g):
    bsz, seq, _ = xbc.shape
    f32 = jnp.float32
    pad = jnp.pad(xbc, ((0, 0), (SSD_CONV - 1, 0), (0, 0)))
    conv = conv_b + sum(pad[:, k:k + seq] * conv_w[k] for k in range(SSD_CONV))
    xbc = jax.nn.silu(conv).astype(f32)
    xs, bm, cm = jnp.split(xbc, [D_SSD, D_SSD + SSD_GROUPS * SSD_STATE], axis=-1)
    xs = xs.reshape(bsz, seq, SSD_GROUPS, SSD_HPG, SSD_HEADDIM)
    bm = bm.reshape(bsz, seq, SSD_GROUPS, SSD_STATE)
    cm = cm.reshape(bsz, seq, SSD_GROUPS, SSD_STATE)
    dt = jax.nn.softplus(dt_raw.astype(f32) + dt_bias.astype(f32))
    dt = dt.reshape(bsz, seq, SSD_GROUPS, SSD_HPG)
    a = -jnp.exp(a_log.astype(f32)).reshape(SSD_GROUPS, SSD_HPG)
    y = ssd_chunked(xs * dt[..., None], dt * a, bm, cm)
    y = y + d_skip.astype(f32).reshape(SSD_GROUPS, SSD_HPG, 1) * xs
    y = y.reshape(bsz, seq, D_SSD).astype(z.dtype)
    return rms_norm(y * jax.nn.silu(z), norm_g)


def cross_attention(h, m, wq, wk, wv, wo):
    bsz, seq, _ = h.shape
    q = (h @ wq).reshape(bsz, seq, XA_HEADS, XA_HEAD_DIM)
    k = (m @ wk).reshape(bsz, -1, XA_HEADS, XA_HEAD_DIM)
    v = (m @ wv).reshape(bsz, -1, XA_HEADS, XA_HEAD_DIM)
    scores = jnp.einsum('bshd,bmhd->bhsm', q, k).astype(jnp.float32) * (XA_HEAD_DIM ** -0.5)
    probs = jax.nn.softmax(scores, axis=-1).astype(v.dtype)
    o = jnp.einsum('bhsm,bmhd->bshd', probs, v).reshape(bsz, seq, D_MODEL)
    return o @ wo


def setup_inputs(seed: int = 0) -> dict:
    key = jax.random.key(seed)
    ks = jax.random.split(key, 32)
    nrm = jax.random.normal
    f32 = jnp.float32
    Lr = DEPTH
    n_idx = jnp.arange(S5_STATE, dtype=f32)
    log_dt_lo, log_dt_hi = math.log(1e-3), math.log(1e-1)
    ssd_dt = jnp.exp(jax.random.uniform(ks[12], (Lr, SSD_HEADS), f32, log_dt_lo, log_dt_hi))
    return {
        "x": nrm(ks[0], (BATCH, SEQ, D_MODEL), f32),
        "mem": nrm(ks[1], (BATCH, MEM_LEN, D_MODEL), f32),
        "norm_mix": 1.0 + 0.01 * nrm(ks[2], (Lr, D_MODEL), f32),
        "w_in": nrm(ks[3], (Lr, D_MODEL, D_IN_PROJ), f32) * D_MODEL ** -0.5,
        "s5_a_re": -0.5 + 0.01 * nrm(ks[4], (Lr, S5_GROUPS, S5_STATE), f32),
        "s5_a_im": math.pi * n_idx + 0.01 * nrm(ks[5], (Lr, S5_GROUPS, S5_STATE), f32),
        "s5_log_dt": jax.random.uniform(ks[6], (Lr, S5_GROUPS), f32, log_dt_lo, log_dt_hi),
        "s5_b_re": nrm(ks[7], (Lr, S5_GROUPS, S5_STATE, S5_GROUP), f32) * (2 * S5_GROUP) ** -0.5,
        "s5_b_im": nrm(ks[8], (Lr, S5_GROUPS, S5_STATE, S5_GROUP), f32) * (2 * S5_GROUP) ** -0.5,
        "s5_c_re": nrm(ks[9], (Lr, S5_GROUPS, S5_GROUP, S5_STATE), f32) * S5_STATE ** -0.5,
        "s5_c_im": nrm(ks[10], (Lr, S5_GROUPS, S5_GROUP, S5_STATE), f32) * S5_STATE ** -0.5,
        "s5_d": nrm(ks[11], (Lr, S5_GROUPS, S5_GROUP), f32),
        "s5_w_glu": nrm(ks[13], (Lr, D_S5, D_S5), f32) * D_S5 ** -0.5,
        "ssd_conv_w": nrm(ks[14], (Lr, SSD_CONV, D_CONV_CH), f32) * SSD_CONV ** -0.5,
        "ssd_conv_b": 0.01 * nrm(ks[15], (Lr, D_CONV_CH), f32),
        "ssd_dt_bias": ssd_dt + jnp.log(-jnp.expm1(-ssd_dt)),
        "ssd_a_log": jnp.log(jax.random.uniform(ks[16], (Lr, SSD_HEADS), f32, 1.0, 16.0)),
        "ssd_d": 1.0 + 0.01 * nrm(ks[17], (Lr, SSD_HEADS), f32),
        "ssd_norm": 1.0 + 0.01 * nrm(ks[18], (Lr, D_SSD), f32),
        "w_out": nrm(ks[19], (Lr, D_MIX, D_MODEL), f32) * D_MIX ** -0.5,
        "norm_xattn": 1.0 + 0.01 * nrm(ks[20], (Lr, D_MODEL), f32),
        "norm_mem": 1.0 + 0.01 * nrm(ks[21], (Lr, D_MODEL), f32),
        "xa_wq": nrm(ks[22], (Lr, D_MODEL, D_MODEL), f32) * D_MODEL ** -0.5,
        "xa_wk": nrm(ks[23], (Lr, D_MODEL, D_MODEL), f32) * D_MODEL ** -0.5,
        "xa_wv": nrm(ks[24], (Lr, D_MODEL, D_MODEL), f32) * D_MODEL ** -0.5,
        "xa_wo": nrm(ks[25], (Lr, D_MODEL, D_MODEL), f32) * D_MODEL ** -0.5,
        "norm_mlp": 1.0 + 0.01 * nrm(ks[26], (Lr, D_MODEL), f32),
        "mlp_w1": nrm(ks[27], (Lr, D_MODEL, D_FF), f32) * D_MODEL ** -0.5,
        "mlp_w2": nrm(ks[28], (Lr, D_FF, D_MODEL), f32) * D_FF ** -0.5,
        "norm_final": 1.0 + 0.01 * nrm(ks[29], (D_MODEL,), f32),
    }


def reference(x, mem, norm_mix, w_in, s5_a_re, s5_a_im, s5_log_dt, s5_b_re, s5_b_im,
              s5_c_re, s5_c_im, s5_d, s5_w_glu, ssd_conv_w, ssd_conv_b, ssd_dt_bias,
              ssd_a_log, ssd_d, ssd_norm, w_out, norm_xattn, norm_mem, xa_wq, xa_wk,
              xa_wv, xa_wo, norm_mlp, mlp_w1, mlp_w2, norm_final):
    splits = [D_S5, D_S5 + D_SSD, D_S5 + D_SSD + D_CONV_CH]
    for l in range(DEPTH):
        h = rms_norm(x, norm_mix[l])
        proj = h @ w_in[l]
        u, z, xbc, dt_raw = jnp.split(proj, splits, axis=-1)
        y_s5 = s5_mixer(u, s5_a_re[l], s5_a_im[l], s5_log_dt[l], s5_b_re[l], s5_b_im[l],
                        s5_c_re[l], s5_c_im[l], s5_d[l], s5_w_glu[l])
        y_ssd = ssd_mixer(z, xbc, dt_raw, ssd_conv_w[l], ssd_conv_b[l], ssd_dt_bias[l],
                          ssd_a_log[l], ssd_d[l], ssd_norm[l])
        x = x + jnp.concatenate([y_s5, y_ssd], axis=-1) @ w_out[l]
        h = rms_norm(x, norm_xattn[l])
        m = rms_norm(mem, norm_mem[l])
        x = x + cross_attention(h, m, xa_wq[l], xa_wk[l], xa_wv[l], xa_wo[l])
        h = rms_norm(x, norm_mlp[l])
        x = x + jnp.square(jax.nn.relu(h @ mlp_w1[l])) @ mlp_w2[l]
    return rms_norm(x, norm_final)
```

```python
import functools
import math

import jax
import jax.numpy as jnp
from jax import lax
from jax.experimental import pallas as pl
from jax.experimental.pallas import tpu as pltpu

F32 = jnp.float32
BF16 = jnp.bfloat16
EPS = 1e-5

LANES = 128
SUBLANES = 8

S5_GROUP = 16
S5_STATE = 64
SSD_HEADDIM = 64
SSD_GROUPS = 4
SSD_STATE = 128
SSD_CONV = 4
SSD_CHUNK = 128
XA_HEADS = 4

S5_TILE = 512
S5_SEG = S5_TILE // SUBLANES
S5_PITCH = S5_SEG + SUBLANES
S5_ROWS = SUBLANES * S5_PITCH
S5_CB = 8
S5_LB = 4

TOK_TILE = 512
VMEM_LIMIT = 56 * 1024 * 1024


def _cparams(sem):
    return pltpu.CompilerParams(dimension_semantics=sem, vmem_limit_bytes=VMEM_LIMIT)


def _rms(x, g):
    return x * lax.rsqrt(jnp.mean(x * x, axis=-1, keepdims=True) + EPS) * g


def _dot(a, b):
    return jnp.dot(a, b, preferred_element_type=F32)


def _dot_nt(a, b):
    return lax.dot_general(a, b, (((1,), (1,)), ((), ())), preferred_element_type=F32)


def _sigmoid(x):
    return 1.0 / (1.0 + jnp.exp(-x))


def _s5_prep_kernel(arb_ref, aib_ref, ldb_ref, br_ref, bi_ref, ar_ref, ai_ref, ld_ref,
                    bbr_ref, bbi_ref, tab_ref):
    def disc(ar, ai, ld):
        dt = jnp.exp(ld)
        mag = jnp.exp(dt * ar)
        return mag * jnp.cos(dt * ai), mag * jnp.sin(dt * ai)

    ar, ai = arb_ref[...], aib_ref[...]
    abr, abi = disc(ar, ai, ldb_ref[...])
    den = ar * ar + ai * ai
    zr, zi = abr - 1.0, abi
    fr = (zr * ar + zi * ai) / den
    fi = (zi * ar - zr * ai) / den
    br, bi = br_ref[...], bi_ref[...]
    bbr_ref[...] = fr * br - fi * bi
    bbi_ref[...] = fr * bi + fi * br

    pr, pi = disc(ar_ref[...], ai_ref[...], ld_ref[...])
    tab_ref[0], tab_ref[1] = pr, pi
    for _ in range(int(math.log2(S5_SEG))):
        pr, pi = pr * pr - pi * pi, 2.0 * pr * pi
    tab_ref[2], tab_ref[3] = pr, pi
    pr, pi = pr * pr - pi * pi, 2.0 * pr * pi
    tab_ref[4], tab_ref[5] = pr, pi
    pr, pi = pr * pr - pi * pi, 2.0 * pr * pi
    tab_ref[6], tab_ref[7] = pr, pi


def _s5_prep(a_re, a_im, log_dt, b_re, b_im, c_re, c_im):
    g, p = a_re.shape
    h = b_re.shape[-1]
    rep = lambda v: jnp.repeat(v, h, axis=-1)
    ldt = jnp.broadcast_to(log_dt[:, None], (g, p))
    nlb = g * p // LANES
    flat = lambda v: v.reshape(nlb, LANES)
    bbr, bbi, tab = pl.pallas_call(
        _s5_prep_kernel,
        out_shape=(jax.ShapeDtypeStruct((g, p * h), F32),
                   jax.ShapeDtypeStruct((g, p * h), F32),
                   jax.ShapeDtypeStruct((8, nlb, LANES), F32)),
        name="s5_prep",
    )(rep(a_re), rep(a_im), rep(ldt), b_re.reshape(g, p * h), b_im.reshape(g, p * h),
      flat(a_re), flat(a_im), flat(ldt))
    ncb = g // 8
    eye = jnp.eye(8, dtype=F32)
    bb = jnp.stack([bbr, bbi]).reshape(2, ncb, 8, p, h)
    bblk = jnp.einsum('rcgph,gk->cghrkp', bb, eye).reshape(ncb, 8 * h, 2 * 8 * p)
    cc = jnp.stack([c_re, -c_im]).reshape(2, ncb, 8, h, p)
    cblk = jnp.einsum('rcghp,gk->crkpgh', cc, eye).reshape(ncb, 2 * 8 * p, 8 * h)
    tab = jnp.broadcast_to(tab[:, :, None, :], (8, nlb, SUBLANES, LANES))
    return bblk.astype(BF16), cblk.astype(BF16), tab


def _inproj_kernel(x_ref, g_ref, w_ref, u_ref, z_ref, xbc_ref, dt_ref, *, d_s5, d_ssd, d_conv):
    h = _rms(x_ref[...], g_ref[...]).astype(BF16)
    for c in range(d_s5 // LANES):
        u_ref[c] = _dot(h, w_ref[:, c * LANES:(c + 1) * LANES])
    o = d_s5
    z_ref[...] = _dot(h, w_ref[:, o:o + d_ssd])
    o += d_ssd
    xbc_ref[...] = _dot(h, w_ref[:, o:o + d_conv])
    o += d_conv
    dt_ref[...] = _dot(h, w_ref[:, o:o + LANES])


def _inproj(x, g, w, d_s5, d_ssd, d_conv):
    b, s, d = x.shape
    n = w.shape[1]
    tm = TOK_TILE
    ncb = d_s5 // LANES
    return pl.pallas_call(
        functools.partial(_inproj_kernel, d_s5=d_s5, d_ssd=d_ssd, d_conv=d_conv),
        grid=(b, s // tm),
        in_specs=[pl.BlockSpec((None, tm, d), lambda i, j: (i, j, 0)),
                  pl.BlockSpec((1, d), lambda i, j: (0, 0)),
                  pl.BlockSpec((d, n), lambda i, j: (0, 0))],
        out_specs=(pl.BlockSpec((None, ncb, tm, LANES), lambda i, j: (i, 0, j, 0)),
                   pl.BlockSpec((None, tm, d_ssd), lambda i, j: (i, j, 0)),
                   pl.BlockSpec((None, tm, d_conv), lambda i, j: (i, j, 0)),
                   pl.BlockSpec((None, tm, LANES), lambda i, j: (i, j, 0))),
        out_shape=(jax.ShapeDtypeStruct((b, ncb, s, LANES), F32),
                   jax.ShapeDtypeStruct((b, s, d_ssd), F32),
                   jax.ShapeDtypeStruct((b, s, d_conv), F32),
                   jax.ShapeDtypeStruct((b, s, LANES), F32)),
        compiler_params=_cparams(("arbitrary", "arbitrary")),
        name="inproj",
    )(x, g, w)


def _s5_kernel(u_ref, bblk_ref, cblk_ref, tab_ref, dskip_ref, wglu_ref, y_ref,
               drv, carry, ybuf):
    @pl.when(pl.program_id(1) == 0)
    def _():
        carry[...] = jnp.zeros_like(carry)

    row = lax.broadcasted_iota(jnp.int32, (SUBLANES, LANES), 0)

    def shift_down(v, k):
        return jnp.where(row >= k, pltpu.roll(v, k, 0), 0.0)

    def cmul(ar, ai, xr, xi):
        return ar * xr - ai * xi, ar * xi + ai * xr

    def per_cb(cb, _):
        ub = u_ref[cb]
        d = _dot(ub.astype(BF16), bblk_ref[cb])
        for j in range(SUBLANES):
            for k in range(2 * S5_LB):
                drv[k, j * S5_PITCH:j * S5_PITCH + S5_SEG, :] = (
                    d[j * S5_SEG:(j + 1) * S5_SEG, k * LANES:(k + 1) * LANES])

        lbs = [cb * S5_LB + k for k in range(S5_LB)]
        ar = [tab_ref[0, lb] for lb in lbs]
        ai = [tab_ref[1, lb] for lb in lbs]

        def step(i, st, store):
            out = []
            for k in range(S5_LB):
                sr, si = st[2 * k], st[2 * k + 1]
                idx = pl.ds(i, SUBLANES, stride=S5_PITCH)
                dr, di = drv[k, idx, :], drv[S5_LB + k, idx, :]
                pr, pi = cmul(ar[k], ai[k], sr, si)
                nr, ni = pr + dr, pi + di
                if store:
                    drv[k, idx, :] = nr
                    drv[S5_LB + k, idx, :] = ni
                out += [nr, ni]
            return tuple(out)

        zero = jnp.zeros((SUBLANES, LANES), F32)
        ends = lax.fori_loop(0, S5_SEG, lambda i, st: step(i, st, False),
                             (zero,) * (2 * S5_LB), unroll=4)
        init = []
        for k in range(S5_LB):
            lb = lbs[k]
            er, ei = ends[2 * k], ends[2 * k + 1]
            cr, ci = carry[0, lb], carry[1, lb]
            xr = jnp.where(row == 0, cr, shift_down(er, 1))
            xi = jnp.where(row == 0, ci, shift_down(ei, 1))
            for lvl, sh in ((2, 1), (4, 2), (6, 4)):
                pr, pi = cmul(tab_ref[lvl, lb], tab_ref[lvl + 1, lb],
                              shift_down(xr, sh), shift_down(xi, sh))
                xr, xi = xr + pr, xi + pi
            pr, pi = cmul(tab_ref[2, lb], tab_ref[3, lb], xr, xi)
            nr, ni = pr + er, pi + ei
            carry[0, lb] = jnp.broadcast_to(nr[SUBLANES - 1:SUBLANES, :], (SUBLANES, LANES))
            carry[1, lb] = jnp.broadcast_to(ni[SUBLANES - 1:SUBLANES, :], (SUBLANES, LANES))
            init += [xr, xi]
        lax.fori_loop(0, S5_SEG, lambda i, st: step(i, st, True), tuple(init), unroll=4)

        st = jnp.concatenate(
            [jnp.concatenate([drv[k, j * S5_PITCH:j * S5_PITCH + S5_SEG, :]
                              for j in range(SUBLANES)], axis=0).astype(BF16)
             for k in range(2 * S5_LB)], axis=1)
        ybuf[cb] = _dot(st, cblk_ref[cb]) + dskip_ref[cb] * ub
        return 0

    lax.fori_loop(0, S5_CB, per_cb, 0)
    y = jnp.concatenate([ybuf[c] for c in range(S5_CB)], axis=1)
    y = jax.nn.gelu(y, approximate=True)
    y_ref[...] = y * _sigmoid(_dot(y.astype(BF16), wglu_ref[...]))


def _s5(u, bblk, cblk, tab, dskip, wglu):
    b, ncb, s, _ = u.shape
    d = ncb * LANES
    nlb = tab.shape[1]
    const = lambda shape: pl.BlockSpec(shape, lambda i, j: (0,) * len(shape))
    return pl.pallas_call(
        _s5_kernel,
        grid=(b, s // S5_TILE),
        in_specs=[pl.BlockSpec((None, ncb, S5_TILE, LANES), lambda i, j: (i, 0, j, 0)),
                  const(bblk.shape), const(cblk.shape), const(tab.shape),
                  const(dskip.shape), const(wglu.shape)],
        out_specs=pl.BlockSpec((None, S5_TILE, d), lambda i, j: (i, j, 0)),
        out_shape=jax.ShapeDtypeStruct((b, s, d), F32),
        scratch_shapes=[pltpu.VMEM((2 * S5_LB, S5_ROWS, LANES), F32),
                        pltpu.VMEM((2, nlb, SUBLANES, LANES), F32),
                        pltpu.VMEM((ncb, S5_TILE, LANES), F32)],
        compiler_params=_cparams(("arbitrary", "arbitrary")),
        name="s5_mixer",
    )(u, bblk, cblk, tab, dskip, wglu)


def _split3(v):
    hi = v.astype(BF16)
    r = v - hi.astype(F32)
    mid = r.astype(BF16)
    lo = (r - mid.astype(F32)).astype(BF16)
    return hi, mid, lo


def _ssd_kernel(z_ref, xbc_ref, dt_ref, cw_ref, cbias_ref, dtb_ref, alog_ref, dexp_ref,
                ng_ref, y_ref, xpad, state, *, d_ssd, nheads):
    L = SSD_CHUNK
    P = SSD_HEADDIM
    N = SSD_STATE
    hpg = nheads // SSD_GROUPS
    gw = hpg * P

    @pl.when(pl.program_id(1) == 0)
    def _():
        xpad[0:SUBLANES, :] = jnp.zeros((SUBLANES, xpad.shape[1]), F32)
        state[...] = jnp.zeros_like(state)

    xpad[SUBLANES:SUBLANES + L, :] = xbc_ref[...]
    conv = cbias_ref[...]
    for k in range(SSD_CONV):
        off = SUBLANES - (SSD_CONV - 1) + k
        conv = conv + cw_ref[k:k + 1, :] * xpad[off:off + L, :]
    xpad[0:SUBLANES, :] = xpad[L:L + SUBLANES, :]
    xa = conv * _sigmoid(conv)
    xs = xa[:, :d_ssd]
    bm = xa[:, d_ssd:d_ssd + SSD_GROUPS * N]
    cm = xa[:, d_ssd + SSD_GROUPS * N:]

    t = dt_ref[...] + dtb_ref[...]
    dt = jnp.maximum(t, 0.0) + jnp.log(1.0 + jnp.exp(-jnp.abs(t)))
    a = dt * (-jnp.exp(alog_ref[...]))

    ri = lax.broadcasted_iota(jnp.int32, (L, L), 0)
    ci = lax.broadcasted_iota(jnp.int32, (L, L), 1)
    causal = ri >= ci
    ltri = causal.astype(BF16)
    utri = (ri <= ci).astype(BF16)
    acum = sum(_dot(ltri, part) for part in _split3(a))
    acum_t = sum(_dot(part, utri) for part in _split3(a.T))
    alast = acum[L - 1:L, :]
    ea = jnp.exp(acum)
    wdec = dt * jnp.exp(alast - acum)
    elast = jnp.exp(alast)

    def expand(v):
        rows = v.shape[0]
        return jnp.concatenate(
            [jnp.broadcast_to(v[:, r:r + 1], (rows, P)) for r in range(nheads)], axis=1)

    xdt = (xs * expand(dt)).astype(BF16)
    xw = (xs * expand(wdec)).astype(BF16)
    ea_x = expand(ea)
    elast_x = expand(elast)

    ys = []
    for g in range(SSD_GROUPS):
        bg = bm[:, g * N:(g + 1) * N].astype(BF16)
        cg = cm[:, g * N:(g + 1) * N].astype(BF16)
        cb = _dot_nt(cg, bg)
        st = state[g]
        y_off = _dot(cg, st.astype(BF16)) * ea_x[:, g * gw:(g + 1) * gw]
        yd = []
        for rr in range(hpg):
            r = g * hpg + rr
            diff = acum[:, r:r + 1] - acum_t[r:r + 1, :]
            m = (cb * jnp.exp(jnp.where(causal, diff, -1e30))).astype(BF16)
            yd.append(_dot(m, xdt[:, r * P:(r + 1) * P]))
        ys.append(jnp.concatenate(yd, axis=1) + y_off)
        state[g] = st * elast_x[:, g * gw:(g + 1) * gw] + _dot(bg.T, xw[:, g * gw:(g + 1) * gw])
    y = jnp.concatenate(ys, axis=1) + dexp_ref[...] * xs
    z = z_ref[...]
    y_ref[...] = _rms(y * (z * _sigmoid(z)), ng_ref[...])


def _ssd(z, xbc, dtp, conv_w, conv_b, dt_bias, a_log, dexp, norm_g):
    b, s, d_ssd = z.shape
    d_conv = xbc.shape[-1]
    nheads = d_ssd // SSD_HEADDIM
    L = SSD_CHUNK
    const = lambda shape: pl.BlockSpec(shape, lambda i, j: (0,) * len(shape))
    return pl.pallas_call(
        functools.partial(_ssd_kernel, d_ssd=d_ssd, nheads=nheads),
        grid=(b, s // L),
        in_specs=[pl.BlockSpec((None, L, d_ssd), lambda i, j: (i, j, 0)),
                  pl.BlockSpec((None, L, d_conv), lambda i, j: (i, j, 0)),
                  pl.BlockSpec((None, L, LANES), lambda i, j: (i, j, 0)),
                  const(conv_w.shape), const(conv_b.shape), const(dt_bias.shape),
                  const(a_log.shape), const(dexp.shape), const(norm_g.shape)],
        out_specs=pl.BlockSpec((None, L, d_ssd), lambda i, j: (i, j, 0)),
        out_shape=jax.ShapeDtypeStruct((b, s, d_ssd), F32),
        scratch_shapes=[pltpu.VMEM((L + SUBLANES, d_conv), F32),
                        pltpu.VMEM((SSD_GROUPS, SSD_STATE, d_ssd // SSD_GROUPS), F32)],
        compiler_params=_cparams(("arbitrary", "arbitrary")),
        name="ssd_mixer",
    )(z, xbc, dtp, conv_w, conv_b, dt_bias, a_log, dexp, norm_g)


def _kv_kernel(mem_ref, g_ref, wk_ref, wv_ref, k_ref, v_ref):
    m = _rms(mem_ref[...], g_ref[...]).astype(BF16)
    k_ref[...] = _dot(m, wk_ref[...]).astype(BF16)
    v_ref[...] = _dot(m, wv_ref[...]).astype(BF16)


def _kv(mem, norm_mem, wk, wv):
    b, m, d = mem.shape
    nl = wk.shape[0]
    out = jax.ShapeDtypeStruct((nl, b, m, d), BF16)
    return pl.pallas_call(
        _kv_kernel,
        grid=(nl, b),
        in_specs=[pl.BlockSpec((None, m, d), lambda l, i: (i, 0, 0)),
                  pl.BlockSpec((None, 1, d), lambda l, i: (l, 0, 0)),
                  pl.BlockSpec((None, d, d), lambda l, i: (l, 0, 0)),
                  pl.BlockSpec((None, d, d), lambda l, i: (l, 0, 0))],
        out_specs=(pl.BlockSpec((None, None, m, d), lambda l, i: (l, i, 0, 0)),
                   pl.BlockSpec((None, None, m, d), lambda l, i: (l, i, 0, 0))),
        out_shape=(out, out),
        compiler_params=_cparams(("arbitrary", "arbitrary")),
        name="mem_kv",
    )(mem, norm_mem, wk, wv)


def _outproj_kernel(x_ref, y1_ref, y2_ref, w_ref, o_ref):
    d1 = y1_ref.shape[-1]
    acc = _dot(y1_ref[...].astype(BF16), w_ref[:d1, :])
    acc += _dot(y2_ref[...].astype(BF16), w_ref[d1:, :])
    o_ref[...] = x_ref[...] + acc


def _outproj(x, y1, y2, w):
    b, s, d = x.shape
    tm = TOK_TILE
    tile = lambda n: pl.BlockSpec((None, tm, n), lambda i, j: (i, j, 0))
    return pl.pallas_call(
        _outproj_kernel,
        grid=(b, s // tm),
        in_specs=[tile(d), tile(y1.shape[-1]), tile(y2.shape[-1]),
                  pl.BlockSpec(w.shape, lambda i, j: (0, 0))],
        out_specs=tile(d),
        out_shape=jax.ShapeDtypeStruct(x.shape, F32),
        compiler_params=_cparams(("arbitrary", "arbitrary")),
        name="outproj",
    )(x, y1, y2, w)


def _xattn_kernel(x_ref, g_ref, wq_ref, wo_ref, k_ref, v_ref, o_ref):
    x = x_ref[...]
    d = x.shape[-1]
    hd = d // XA_HEADS
    h = _rms(x, g_ref[...]).astype(BF16)
    q = (_dot(h, wq_ref[...]) * (hd ** -0.5)).astype(BF16)
    outs = []
    for a in range(XA_HEADS):
        sl = slice(a * hd, (a + 1) * hd)
        s = _dot_nt(q[:, sl], k_ref[:, sl])
        e = jnp.exp(s - jnp.max(s, axis=-1, keepdims=True))
        p = e / jnp.sum(e, axis=-1, keepdims=True)
        outs.append(_dot(p.astype(BF16), v_ref[:, sl]))
    o = jnp.concatenate(outs, axis=1).astype(BF16)
    o_ref[...] = x + _dot(o, wo_ref[...])


def _xattn(x, g, wq, wo, k, v):
    b, s, d = x.shape
    m = k.shape[1]
    tm = TOK_TILE
    tile = pl.BlockSpec((None, tm, d), lambda i, j: (i, j, 0))
    return pl.pallas_call(
        _xattn_kernel,
        grid=(b, s // tm),
        in_specs=[tile, pl.BlockSpec((1, d), lambda i, j: (0, 0)),
                  pl.BlockSpec((d, d), lambda i, j: (0, 0)),
                  pl.BlockSpec((d, d), lambda i, j: (0, 0)),
                  pl.BlockSpec((None, m, d), lambda i, j: (i, 0, 0)),
                  pl.BlockSpec((None, m, d), lambda i, j: (i, 0, 0))],
        out_specs=tile,
        out_shape=jax.ShapeDtypeStruct(x.shape, F32),
        compiler_params=_cparams(("arbitrary", "arbitrary")),
        name="xattn",
    )(x, g, wq, wo, k, v)


def _mlp_kernel(x_ref, g_ref, w1_ref, w2_ref, gf_ref, o_ref, *, final_norm):
    x = x_ref[...]
    d = x.shape[-1]
    h = _rms(x, g_ref[...]).astype(BF16)
    acc = x
    for c in range(w1_ref.shape[1] // d):
        a = jnp.maximum(_dot(h, w1_ref[:, c * d:(c + 1) * d]), 0.0)
        acc = acc + _dot((a * a).astype(BF16), w2_ref[c * d:(c + 1) * d, :])
    o_ref[...] = _rms(acc, gf_ref[...]) if final_norm else acc


def _mlp(x, g, w1, w2, gf, final_norm):
    b, s, d = x.shape
    tm = TOK_TILE
    tile = pl.BlockSpec((None, tm, d), lambda i, j: (i, j, 0))
    vec = pl.BlockSpec((1, d), lambda i, j: (0, 0))
    return pl.pallas_call(
        functools.partial(_mlp_kernel, final_norm=final_norm),
        grid=(b, s // tm),
        in_specs=[tile, vec, pl.BlockSpec(w1.shape, lambda i, j: (0, 0)),
                  pl.BlockSpec(w2.shape, lambda i, j: (0, 0)), vec],
        out_specs=tile,
        out_shape=jax.ShapeDtypeStruct(x.shape, F32),
        compiler_params=_cparams(("arbitrary", "arbitrary")),
        name="mlp",
    )(x, g, w1, w2, gf)


def kernel(x, mem, norm_mix, w_in, s5_a_re, s5_a_im, s5_log_dt, s5_b_re, s5_b_im, s5_c_re, s5_c_im, s5_d, s5_w_glu, ssd_conv_w, ssd_conv_b, ssd_dt_bias, ssd_a_log, ssd_d, ssd_norm, w_out, norm_xattn, norm_mem, xa_wq, xa_wk, xa_wv, xa_wo, norm_mlp, mlp_w1, mlp_w2, norm_final):
    depth = w_in.shape[0]
    d_s5 = s5_w_glu.shape[-1]
    d_ssd = ssd_norm.shape[-1]
    d_conv = ssd_conv_w.shape[-1]
    nheads = ssd_dt_bias.shape[-1]
    row = lambda v: v.reshape(1, -1)
    padl = lambda v: jnp.pad(v, ((0, 0), (0, LANES - v.shape[-1])))

    k_all, v_all = _kv(mem, norm_mem[:, None, :], xa_wk.astype(BF16), xa_wv.astype(BF16))
    for l in range(depth):
        w = jnp.pad(w_in[l], ((0, 0), (0, LANES - nheads))).astype(BF16)
        u, z, xbc, dtp = _inproj(x, row(norm_mix[l]), w, d_s5, d_ssd, d_conv)
        bblk, cblk, tab = _s5_prep(s5_a_re[l], s5_a_im[l], s5_log_dt[l], s5_b_re[l], s5_b_im[l],
                                   s5_c_re[l], s5_c_im[l])
        y_s5 = _s5(u, bblk, cblk, tab, s5_d[l].reshape(-1, 1, LANES), s5_w_glu[l].astype(BF16))
        y_ssd = _ssd(z, xbc, dtp, ssd_conv_w[l], row(ssd_conv_b[l]), padl(row(ssd_dt_bias[l])),
                     padl(row(ssd_a_log[l])), row(jnp.repeat(ssd_d[l], SSD_HEADDIM)),
                     row(ssd_norm[l]))
        x = _outproj(x, y_s5, y_ssd, w_out[l].astype(BF16))
        x = _xattn(x, row(norm_xattn[l]), xa_wq[l].astype(BF16), xa_wo[l].astype(BF16),
                   k_all[l], v_all[l])
        x = _mlp(x, row(norm_mlp[l]), mlp_w1[l].astype(BF16), mlp_w2[l].astype(BF16),
                 row(norm_final), l == depth - 1)
    return x
```

```python
import functools
import math

import jax
import jax.numpy as jnp
from jax import lax
from jax.experimental import pallas as pl
from jax.experimental.pallas import tpu as pltpu

F32 = jnp.float32
BF16 = jnp.bfloat16
EPS = 1e-5

LANES = 128
SUBLANES = 8

S5_GROUP = 16
S5_STATE = 64
SSD_HEADDIM = 64
SSD_GROUPS = 4
SSD_STATE = 128
SSD_CONV = 4
SSD_CHUNK = 128
XA_HEADS = 4

S5_TILE = 512
S5_SEG = S5_TILE // SUBLANES
S5_PITCH = S5_SEG + SUBLANES
S5_ROWS = SUBLANES * S5_PITCH
S5_CB = 8
S5_LB = 4

TOK_TILE = 512
VMEM_LIMIT = 56 * 1024 * 1024


def _cparams(sem):
    return pltpu.CompilerParams(dimension_semantics=sem, vmem_limit_bytes=VMEM_LIMIT)


def _rms(x, g):
    return x * lax.rsqrt(jnp.mean(x * x, axis=-1, keepdims=True) + EPS) * g


def _dot(a, b):
    return jnp.dot(a, b, preferred_element_type=F32)


def _dot_nt(a, b):
    return lax.dot_general(a, b, (((1,), (1,)), ((), ())), preferred_element_type=F32)


def _sigmoid(x):
    return 1.0 / (1.0 + jnp.exp(-x))


def _s5_prep_kernel(arb_ref, aib_ref, ldb_ref, br_ref, bi_ref, ar_ref, ai_ref, ld_ref,
                    bbr_ref, bbi_ref, tab_ref):
    def disc(ar, ai, ld):
        dt = jnp.exp(ld)
        mag = jnp.exp(dt * ar)
        return mag * jnp.cos(dt * ai), mag * jnp.sin(dt * ai)

    ar, ai = arb_ref[...], aib_ref[...]
    abr, abi = disc(ar, ai, ldb_ref[...])
    den = ar * ar + ai * ai
    zr, zi = abr - 1.0, abi
    fr = (zr * ar + zi * ai) / den
    fi = (zi * ar - zr * ai) / den
    br, bi = br_ref[...], bi_ref[...]
    bbr_ref[...] = fr * br - fi * bi
    bbi_ref[...] = fr * bi + fi * br

    pr, pi = disc(ar_ref[...], ai_ref[...], ld_ref[...])
    tab_ref[0], tab_ref[1] = pr, pi
    for _ in range(int(math.log2(S5_SEG))):
        pr, pi = pr * pr - pi * pi, 2.0 * pr * pi
    tab_ref[2], tab_ref[3] = pr, pi
    pr, pi = pr * pr - pi * pi, 2.0 * pr * pi
    tab_ref[4], tab_ref[5] = pr, pi
    pr, pi = pr * pr - pi * pi, 2.0 * pr * pi
    tab_ref[6], tab_ref[7] = pr, pi


def _s5_prep(a_re, a_im, log_dt, b_re, b_im, c_re, c_im):
    g, p = a_re.shape
    h = b_re.shape[-1]
    rep = lambda v: jnp.repeat(v, h, axis=-1)
    ldt = jnp.broadcast_to(log_dt[:, None], (g, p))
    nlb = g * p // LANES
    flat = lambda v: v.reshape(nlb, LANES)
    bbr, bbi, tab = pl.pallas_call(
        _s5_prep_kernel,
        out_shape=(jax.ShapeDtypeStruct((g, p * h), F32),
                   jax.ShapeDtypeStruct((g, p * h), F32),
                   jax.ShapeDtypeStruct((8, nlb, LANES), F32)),
        name="s5_prep",
    )(rep(a_re), rep(a_im), rep(ldt), b_re.reshape(g, p * h), b_im.reshape(g, p * h),
      flat(a_re), flat(a_im), flat(ldt))
    ncb = g // 8
    eye = jnp.eye(8, dtype=F32)
    bb = jnp.stack([bbr, bbi]).reshape(2, ncb, 8, p, h)
    bblk = jnp.einsum('rcgph,gk->crghkp', bb, eye).reshape(ncb, 2, 8 * h, 8 * p)
    cc = jnp.stack([c_re, -c_im]).reshape(2, ncb, 8, h, p)
    cblk = jnp.einsum('rcghp,gk->crkpgh', cc, eye).reshape(ncb, 2, 8 * p, 8 * h)
    tab = jnp.broadcast_to(tab[:, :, None, :], (8, nlb, SUBLANES, LANES))
    return bblk.astype(BF16), cblk.astype(BF16), tab


def _inproj_kernel(x_ref, g_ref, w_ref, u_ref, z_ref, xbc_ref, dt_ref, *, d_s5, d_ssd, d_conv):
    h = _rms(x_ref[...], g_ref[...]).astype(BF16)
    for c in range(d_s5 // LANES):
        u_ref[c] = _dot(h, w_ref[:, c * LANES:(c + 1) * LANES])
    o = d_s5
    z_ref[...] = _dot(h, w_ref[:, o:o + d_ssd])
    o += d_ssd
    xbc_ref[...] = _dot(h, w_ref[:, o:o + d_conv])
    o += d_conv
    dt_ref[...] = _dot(h, w_ref[:, o:o + LANES])


def _inproj(x, g, w, d_s5, d_ssd, d_conv):
    b, s, d = x.shape
    n = w.shape[1]
    tm = TOK_TILE
    ncb = d_s5 // LANES
    return pl.pallas_call(
        functools.partial(_inproj_kernel, d_s5=d_s5, d_ssd=d_ssd, d_conv=d_conv),
        grid=(b, s // tm),
        in_specs=[pl.BlockSpec((None, tm, d), lambda i, j: (i, j, 0)),
                  pl.BlockSpec((1, d), lambda i, j: (0, 0)),
                  pl.BlockSpec((d, n), lambda i, j: (0, 0))],
        out_specs=(pl.BlockSpec((None, ncb, tm, LANES), lambda i, j: (i, 0, j, 0)),
                   pl.BlockSpec((None, tm, d_ssd), lambda i, j: (i, j, 0)),
                   pl.BlockSpec((None, tm, d_conv), lambda i, j: (i, j, 0)),
                   pl.BlockSpec((None, tm, LANES), lambda i, j: (i, j, 0))),
        out_shape=(jax.ShapeDtypeStruct((b, ncb, s, LANES), F32),
                   jax.ShapeDtypeStruct((b, s, d_ssd), F32),
                   jax.ShapeDtypeStruct((b, s, d_conv), F32),
                   jax.ShapeDtypeStruct((b, s, LANES), F32)),
        compiler_params=_cparams(("arbitrary", "arbitrary")),
        name="inproj",
    )(x, g, w)


def _s5_kernel(u_ref, bblk_ref, cblk_ref, tab_ref, dskip_ref, wglu_ref, y_ref,
               buf0, buf1, buf2, upad, ubf, yacc, ypad, carry):
    bufs = (buf0, buf1, buf2)
    nsl = 2 * S5_LB
    half = S5_SEG // 2

    @pl.when(pl.program_id(1) == 0)
    def _():
        carry[...] = jnp.zeros_like(carry)

    row = lax.broadcasted_iota(jnp.int32, (SUBLANES, LANES), 0)

    def shift_down(v, k):
        return jnp.where(row >= k, pltpu.roll(v, k, 0), 0.0)

    def cmul(ar, ai, xr, xi):
        return ar * xr - ai * xi, ar * xi + ai * xr

    def prep_u(c):
        for j in range(SUBLANES):
            upad[j * S5_PITCH:j * S5_PITCH + S5_SEG, :] = u_ref[c, j * S5_SEG:(j + 1) * S5_SEG, :]
        ubf[...] = jnp.concatenate(
            [upad[pl.ds(i, SUBLANES, stride=S5_PITCH), :] for i in range(S5_SEG)],
            axis=0).astype(BF16)

    def drive_half(c, it, dst):
        res = _dot(ubf[...], bblk_ref[c, it])
        for k in range(S5_LB):
            dst[S5_LB * it + k] = res[:, k * LANES:(k + 1) * LANES]

    def readout_half(c, it, src):
        lhs = jnp.concatenate([src[S5_LB * it + k] for k in range(S5_LB)], axis=1)
        yacc[...] += _dot(lhs.astype(BF16), cblk_ref[c, it])

    def unpermute(c):
        for i in range(S5_SEG):
            ypad[c, pl.ds(i, SUBLANES, stride=S5_PITCH), :] = yacc[i * SUBLANES:(i + 1) * SUBLANES, :]

    def scan_steps(cur, base, st, ar, ai, store):
        for s in range(half):
            r = pl.ds(pl.multiple_of((base + s) * SUBLANES, SUBLANES), SUBLANES)
            out = []
            for k in range(S5_LB):
                pr, pi = cmul(ar[k], ai[k], st[2 * k], st[2 * k + 1])
                nr, ni = pr + cur[k, r, :], pi + cur[S5_LB + k, r, :]
                if store:
                    cur[k, r, :] = nr
                    cur[S5_LB + k, r, :] = ni
                out += [nr, ni]
            st = tuple(out)
        return st

    prep_u(0)
    for it in range(2):
        drive_half(0, it, bufs[0])

    zero = jnp.zeros((SUBLANES, LANES), F32)
    for cb in range(S5_CB):
        cur, nxt, prv = bufs[cb % 3], bufs[(cb + 1) % 3], bufs[(cb + 2) % 3]
        lbs = [cb * S5_LB + k for k in range(S5_LB)]
        ar = [tab_ref[0, lb] for lb in lbs]
        ai = [tab_ref[1, lb] for lb in lbs]
        has_next, has_prev = cb + 1 < S5_CB, cb > 0

        if has_next:
            prep_u(cb + 1)

        def pass1(it, st, cur=cur, nxt=nxt, ar=ar, ai=ai, cb=cb, has_next=has_next):
            if has_next:
                drive_half(cb + 1, it, nxt)
            return scan_steps(cur, it * half, st, ar, ai, False)

        ends = (zero,) * nsl
        for it in range(2):
            ends = pass1(it, ends)

        init = []
        for k in range(S5_LB):
            lb = lbs[k]
            er, ei = ends[2 * k], ends[2 * k + 1]
            xr = jnp.where(row == 0, carry[0, lb], shift_down(er, 1))
            xi = jnp.where(row == 0, carry[1, lb], shift_down(ei, 1))
            for lvl, sh in ((2, 1), (4, 2), (6, 4)):
                pr, pi = cmul(tab_ref[lvl, lb], tab_ref[lvl + 1, lb],
                              shift_down(xr, sh), shift_down(xi, sh))
                xr, xi = xr + pr, xi + pi
            pr, pi = cmul(tab_ref[2, lb], tab_ref[3, lb], xr, xi)
            nr, ni = pr + er, pi + ei
            carry[0, lb] = jnp.broadcast_to(nr[SUBLANES - 1:SUBLANES, :], (SUBLANES, LANES))
            carry[1, lb] = jnp.broadcast_to(ni[SUBLANES - 1:SUBLANES, :], (SUBLANES, LANES))
            init += [xr, xi]

        if has_prev:
            yacc[...] = jnp.zeros_like(yacc)

        def pass2(it, st, cur=cur, prv=prv, ar=ar, ai=ai, cb=cb, has_prev=has_prev):
            if has_prev:
                readout_half(cb - 1, it, prv)
            return scan_steps(cur, it * half, st, ar, ai, True)

        st = tuple(init)
        for it in range(2):
            st = pass2(it, st)
        if has_prev:
            unpermute(cb - 1)

    last = S5_CB - 1
    yacc[...] = jnp.zeros_like(yacc)
    for it in range(2):
        readout_half(last, it, bufs[last % 3])
    unpermute(last)

    y = jnp.concatenate(
        [jnp.concatenate([ypad[c, j * S5_PITCH:j * S5_PITCH + S5_SEG, :]
                          for j in range(SUBLANES)], axis=0) + dskip_ref[c] * u_ref[c]
         for c in range(S5_CB)], axis=1)
    y = jax.nn.gelu(y, approximate=True)
    y_ref[...] = y * _sigmoid(_dot(y.astype(BF16), wglu_ref[...]))


def _s5(u, bblk, cblk, tab, dskip, wglu):
    b, ncb, s, _ = u.shape
    d = ncb * LANES
    nlb = tab.shape[1]
    const = lambda shape: pl.BlockSpec(shape, lambda i, j: (0,) * len(shape))
    slab = pltpu.VMEM((2 * S5_LB, S5_TILE, LANES), F32)
    return pl.pallas_call(
        _s5_kernel,
        grid=(b, s // S5_TILE),
        in_specs=[pl.BlockSpec((None, ncb, S5_TILE, LANES), lambda i, j: (i, 0, j, 0)),
                  const(bblk.shape), const(cblk.shape), const(tab.shape),
                  const(dskip.shape), const(wglu.shape)],
        out_specs=pl.BlockSpec((None, S5_TILE, d), lambda i, j: (i, j, 0)),
        out_shape=jax.ShapeDtypeStruct((b, s, d), F32),
        scratch_shapes=[slab, slab, slab,
                        pltpu.VMEM((S5_ROWS, LANES), F32),
                        pltpu.VMEM((S5_TILE, LANES), BF16),
                        pltpu.VMEM((S5_TILE, LANES), F32),
                        pltpu.VMEM((ncb, S5_ROWS, LANES), F32),
                        pltpu.VMEM((2, nlb, SUBLANES, LANES), F32)],
        compiler_params=_cparams(("arbitrary", "arbitrary")),
        name="s5_mixer",
    )(u, bblk, cblk, tab, dskip, wglu)


def _split3(v):
    hi = v.astype(BF16)
    r = v - hi.astype(F32)
    mid = r.astype(BF16)
    lo = (r - mid.astype(F32)).astype(BF16)
    return hi, mid, lo


def _ssd_kernel(z_ref, xbc_ref, dt_ref, cw_ref, cbias_ref, dtb_ref, alog_ref, dexp_ref,
                ng_ref, y_ref, xpad, state, *, d_ssd, nheads):
    L = SSD_CHUNK
    P = SSD_HEADDIM
    N = SSD_STATE
    hpg = nheads // SSD_GROUPS
    gw = hpg * P

    @pl.when(pl.program_id(1) == 0)
    def _():
        xpad[0:SUBLANES, :] = jnp.zeros((SUBLANES, xpad.shape[1]), F32)
        state[...] = jnp.zeros_like(state)

    xpad[SUBLANES:SUBLANES + L, :] = xbc_ref[...]
    conv = cbias_ref[...]
    for k in range(SSD_CONV):
        off = SUBLANES - (SSD_CONV - 1) + k
        conv = conv + cw_ref[k:k + 1, :] * xpad[off:off + L, :]
    xpad[0:SUBLANES, :] = xpad[L:L + SUBLANES, :]
    xa = conv * _sigmoid(conv)
    xs = xa[:, :d_ssd]
    bm = xa[:, d_ssd:d_ssd + SSD_GROUPS * N]
    cm = xa[:, d_ssd + SSD_GROUPS * N:]

    t = dt_ref[...] + dtb_ref[...]
    dt = jnp.maximum(t, 0.0) + jnp.log(1.0 + jnp.exp(-jnp.abs(t)))
    a = dt * (-jnp.exp(alog_ref[...]))

    ri = lax.broadcasted_iota(jnp.int32, (L, L), 0)
    ci = lax.broadcasted_iota(jnp.int32, (L, L), 1)
    causal = ri >= ci
    ltri = causal.astype(BF16)
    utri = (ri <= ci).astype(BF16)
    acum = sum(_dot(ltri, part) for part in _split3(a))
    acum_t = sum(_dot(part, utri) for part in _split3(a.T))
    alast = acum[L - 1:L, :]
    ea = jnp.exp(acum)
    wdec = dt * jnp.exp(alast - acum)
    elast = jnp.exp(alast)

    def expand(v):
        rows = v.shape[0]
        return jnp.concatenate(
            [jnp.broadcast_to(v[:, r:r + 1], (rows, P)) for r in range(nheads)], axis=1)

    xdt = (xs * expand(dt)).astype(BF16)
    xw = (xs * expand(wdec)).astype(BF16)
    ea_x = expand(ea)
    elast_x = expand(elast)

    ys = []
    for g in range(SSD_GROUPS):
        bg = bm[:, g * N:(g + 1) * N].astype(BF16)
        cg = cm[:, g * N:(g + 1) * N].astype(BF16)
        cb = _dot_nt(cg, bg)
        st = state[g]
        y_off = _dot(cg, st.astype(BF16)) * ea_x[:, g * gw:(g + 1) * gw]
        yd = []
        for rr in range(hpg):
            r = g * hpg + rr
            diff = acum[:, r:r + 1] - acum_t[r:r + 1, :]
            m = (cb * jnp.exp(jnp.where(causal, diff, -1e30))).astype(BF16)
            yd.append(_dot(m, xdt[:, r * P:(r + 1) * P]))
        ys.append(jnp.concatenate(yd, axis=1) + y_off)
        state[g] = st * elast_x[:, g * gw:(g + 1) * gw] + _dot(bg.T, xw[:, g * gw:(g + 1) * gw])
    y = jnp.concatenate(ys, axis=1) + dexp_ref[...] * xs
    z = z_ref[...]
    y_ref[...] = _rms(y * (z * _sigmoid(z)), ng_ref[...])


def _ssd(z, xbc, dtp, conv_w, conv_b, dt_bias, a_log, dexp, norm_g):
    b, s, d_ssd = z.shape
    d_conv = xbc.shape[-1]
    nheads = d_ssd // SSD_HEADDIM
    L = SSD_CHUNK
    const = lambda shape: pl.BlockSpec(shape, lambda i, j: (0,) * len(shape))
    return pl.pallas_call(
        functools.partial(_ssd_kernel, d_ssd=d_ssd, nheads=nheads),
        grid=(b, s // L),
        in_specs=[pl.BlockSpec((None, L, d_ssd), lambda i, j: (i, j, 0)),
                  pl.BlockSpec((None, L, d_conv), lambda i, j: (i, j, 0)),
                  pl.BlockSpec((None, L, LANES), lambda i, j: (i, j, 0)),
                  const(conv_w.shape), const(conv_b.shape), const(dt_bias.shape),
                  const(a_log.shape), const(dexp.shape), const(norm_g.shape)],
        out_specs=pl.BlockSpec((None, L, d_ssd), lambda i, j: (i, j, 0)),
        out_shape=jax.ShapeDtypeStruct((b, s, d_ssd), F32),
        scratch_shapes=[pltpu.VMEM((L + SUBLANES, d_conv), F32),
                        pltpu.VMEM((SSD_GROUPS, SSD_STATE, d_ssd // SSD_GROUPS), F32)],
        compiler_params=_cparams(("arbitrary", "arbitrary")),
        name="ssd_mixer",
    )(z, xbc, dtp, conv_w, conv_b, dt_bias, a_log, dexp, norm_g)


def _kv_kernel(mem_ref, g_ref, wk_ref, wv_ref, k_ref, v_ref):
    m = _rms(mem_ref[...], g_ref[...]).astype(BF16)
    k_ref[...] = _dot(m, wk_ref[...]).astype(BF16)
    v_ref[...] = _dot(m, wv_ref[...]).astype(BF16)


def _kv(mem, norm_mem, wk, wv):
    b, m, d = mem.shape
    nl = wk.shape[0]
    out = jax.ShapeDtypeStruct((nl, b, m, d), BF16)
    return pl.pallas_call(
        _kv_kernel,
        grid=(nl, b),
        in_specs=[pl.BlockSpec((None, m, d), lambda l, i: (i, 0, 0)),
                  pl.BlockSpec((None, 1, d), lambda l, i: (l, 0, 0)),
                  pl.BlockSpec((None, d, d), lambda l, i: (l, 0, 0)),
                  pl.BlockSpec((None, d, d), lambda l, i: (l, 0, 0))],
        out_specs=(pl.BlockSpec((None, None, m, d), lambda l, i: (l, i, 0, 0)),
                   pl.BlockSpec((None, None, m, d), lambda l, i: (l, i, 0, 0))),
        out_shape=(out, out),
        compiler_params=_cparams(("arbitrary", "arbitrary")),
        name="mem_kv",
    )(mem, norm_mem, wk, wv)


def _outproj_kernel(x_ref, y1_ref, y2_ref, w_ref, o_ref):
    d1 = y1_ref.shape[-1]
    acc = _dot(y1_ref[...].astype(BF16), w_ref[:d1, :])
    acc += _dot(y2_ref[...].astype(BF16), w_ref[d1:, :])
    o_ref[...] = x_ref[...] + acc


def _outproj(x, y1, y2, w):
    b, s, d = x.shape
    tm = TOK_TILE
    tile = lambda n: pl.BlockSpec((None, tm, n), lambda i, j: (i, j, 0))
    return pl.pallas_call(
        _outproj_kernel,
        grid=(b, s // tm),
        in_specs=[tile(d), tile(y1.shape[-1]), tile(y2.shape[-1]),
                  pl.BlockSpec(w.shape, lambda i, j: (0, 0))],
        out_specs=tile(d),
        out_shape=jax.ShapeDtypeStruct(x.shape, F32),
        compiler_params=_cparams(("arbitrary", "arbitrary")),
        name="outproj",
    )(x, y1, y2, w)


def _xattn_kernel(x_ref, g_ref, wq_ref, wo_ref, k_ref, v_ref, o_ref):
    x = x_ref[...]
    d = x.shape[-1]
    hd = d // XA_HEADS
    h = _rms(x, g_ref[...]).astype(BF16)
    q = (_dot(h, wq_ref[...]) * (hd ** -0.5)).astype(BF16)
    outs = []
    for a in range(XA_HEADS):
        sl = slice(a * hd, (a + 1) * hd)
        s = _dot_nt(q[:, sl], k_ref[:, sl])
        e = jnp.exp(s - jnp.max(s, axis=-1, keepdims=True))
        p = e / jnp.sum(e, axis=-1, keepdims=True)
        outs.append(_dot(p.astype(BF16), v_ref[:, sl]))
    o = jnp.concatenate(outs, axis=1).astype(BF16)
    o_ref[...] = x + _dot(o, wo_ref[...])


def _xattn(x, g, wq, wo, k, v):
    b, s, d = x.shape
    m = k.shape[1]
    tm = TOK_TILE
    tile = pl.BlockSpec((None, tm, d), lambda i, j: (i, j, 0))
    return pl.pallas_call(
        _xattn_kernel,
        grid=(b, s // tm),
        in_specs=[tile, pl.BlockSpec((1, d), lambda i, j: (0, 0)),
                  pl.BlockSpec((d, d), lambda i, j: (0, 0)),
                  pl.BlockSpec((d, d), lambda i, j: (0, 0)),
                  pl.BlockSpec((None, m, d), lambda i, j: (i, 0, 0)),
                  pl.BlockSpec((None, m, d), lambda i, j: (i, 0, 0))],
        out_specs=tile,
        out_shape=jax.ShapeDtypeStruct(x.shape, F32),
        compiler_params=_cparams(("arbitrary", "arbitrary")),
        name="xattn",
    )(x, g, wq, wo, k, v)


def _mlp_kernel(x_ref, g_ref, w1_ref, w2_ref, gf_ref, o_ref, *, final_norm):
    x = x_ref[...]
    d = x.shape[-1]
    h = _rms(x, g_ref[...]).astype(BF16)
    acc = x
    for c in range(w1_ref.shape[1] // d):
        a = jnp.maximum(_dot(h, w1_ref[:, c * d:(c + 1) * d]), 0.0)
        acc = acc + _dot((a * a).astype(BF16), w2_ref[c * d:(c + 1) * d, :])
    o_ref[...] = _rms(acc, gf_ref[...]) if final_norm else acc


def _mlp(x, g, w1, w2, gf, final_norm):
    b, s, d = x.shape
    tm = TOK_TILE
    tile = pl.BlockSpec((None, tm, d), lambda i, j: (i, j, 0))
    vec = pl.BlockSpec((1, d), lambda i, j: (0, 0))
    return pl.pallas_call(
        functools.partial(_mlp_kernel, final_norm=final_norm),
        grid=(b, s // tm),
        in_specs=[tile, vec, pl.BlockSpec(w1.shape, lambda i, j: (0, 0)),
                  pl.BlockSpec(w2.shape, lambda i, j: (0, 0)), vec],
        out_specs=tile,
        out_shape=jax.ShapeDtypeStruct(x.shape, F32),
        compiler_params=_cparams(("arbitrary", "arbitrary")),
        name="mlp",
    )(x, g, w1, w2, gf)


def kernel(x, mem, norm_mix, w_in, s5_a_re, s5_a_im, s5_log_dt, s5_b_re, s5_b_im, s5_c_re, s5_c_im, s5_d, s5_w_glu, ssd_conv_w, ssd_conv_b, ssd_dt_bias, ssd_a_log, ssd_d, ssd_norm, w_out, norm_xattn, norm_mem, xa_wq, xa_wk, xa_wv, xa_wo, norm_mlp, mlp_w1, mlp_w2, norm_final):
    depth = w_in.shape[0]
    d_s5 = s5_w_glu.shape[-1]
    d_ssd = ssd_norm.shape[-1]
    d_conv = ssd_conv_w.shape[-1]
    nheads = ssd_dt_bias.shape[-1]
    row = lambda v: v.reshape(1, -1)
    padl = lambda v: jnp.pad(v, ((0, 0), (0, LANES - v.shape[-1])))

    k_all, v_all = _kv(mem, norm_mem[:, None, :], xa_wk.astype(BF16), xa_wv.astype(BF16))
    for l in range(depth):
        w = jnp.pad(w_in[l], ((0, 0), (0, LANES - nheads))).astype(BF16)
        u, z, xbc, dtp = _inproj(x, row(norm_mix[l]), w, d_s5, d_ssd, d_conv)
        bblk, cblk, tab = _s5_prep(s5_a_re[l], s5_a_im[l], s5_log_dt[l], s5_b_re[l], s5_b_im[l],
                                   s5_c_re[l], s5_c_im[l])
        y_s5 = _s5(u, bblk, cblk, tab, s5_d[l].reshape(-1, 1, LANES), s5_w_glu[l].astype(BF16))
        y_ssd = _ssd(z, xbc, dtp, ssd_conv_w[l], row(ssd_conv_b[l]), padl(row(ssd_dt_bias[l])),
                     padl(row(ssd_a_log[l])), row(jnp.repeat(ssd_d[l], SSD_HEADDIM)),
                     row(ssd_norm[l]))
        x = _outproj(x, y_s5, y_ssd, w_out[l].astype(BF16))
        x = _xattn(x, row(norm_xattn[l]), xa_wq[l].astype(BF16), xa_wo[l].astype(BF16),
                   k_all[l], v_all[l])
        x = _mlp(x, row(norm_mlp[l]), mlp_w1[l].astype(BF16), mlp_w2[l].astype(BF16),
                 row(norm_final), l == depth - 1)
    return x
```

```python
import functools
import math

import jax
import jax.numpy as jnp
from jax import lax
from jax.experimental import pallas as pl
from jax.experimental.pallas import tpu as pltpu

F32 = jnp.float32
BF16 = jnp.bfloat16
EPS = 1e-5

LANES = 128
SUBLANES = 8

S5_GROUP = 16
S5_STATE = 64
SSD_HEADDIM = 64
SSD_GROUPS = 4
SSD_STATE = 128
SSD_CONV = 4
SSD_CHUNK = 128
XA_HEADS = 4

S5_TILE = 512
S5_SEG = S5_TILE // SUBLANES
S5_PITCH = S5_SEG + SUBLANES
S5_ROWS = SUBLANES * S5_PITCH
S5_CB = 8
S5_LB = 4

TOK_TILE = 512
VMEM_LIMIT = 56 * 1024 * 1024


def _layer_spec(arr, l):
    nd = arr.ndim - 1
    return pl.BlockSpec((None,) + arr.shape[1:], lambda *_: (l,) + (0,) * nd)


def _cparams(sem):
    return pltpu.CompilerParams(dimension_semantics=sem, vmem_limit_bytes=VMEM_LIMIT)


def _rms(x, g):
    return x * lax.rsqrt(jnp.mean(x * x, axis=-1, keepdims=True) + EPS) * g


def _dot(a, b):
    return jnp.dot(a, b, preferred_element_type=F32)


def _dot_nt(a, b):
    return lax.dot_general(a, b, (((1,), (1,)), ((), ())), preferred_element_type=F32)


def _sigmoid(x):
    return 1.0 / (1.0 + jnp.exp(-x))


def _s5_prep_kernel(arb_ref, aib_ref, ldb_ref, br_ref, bi_ref, ar_ref, ai_ref, ld_ref,
                    bbr_ref, bbi_ref, tab_ref):
    def disc(ar, ai, ld):
        dt = jnp.exp(ld)
        mag = jnp.exp(dt * ar)
        return mag * jnp.cos(dt * ai), mag * jnp.sin(dt * ai)

    ar, ai = arb_ref[...], aib_ref[...]
    abr, abi = disc(ar, ai, ldb_ref[...])
    den = ar * ar + ai * ai
    zr, zi = abr - 1.0, abi
    fr = (zr * ar + zi * ai) / den
    fi = (zi * ar - zr * ai) / den
    br, bi = br_ref[...], bi_ref[...]
    bbr_ref[...] = fr * br - fi * bi
    bbi_ref[...] = fr * bi + fi * br

    pr, pi = disc(ar_ref[...], ai_ref[...], ld_ref[...])
    tab_ref[0], tab_ref[1] = pr, pi
    for _ in range(int(math.log2(S5_SEG))):
        pr, pi = pr * pr - pi * pi, 2.0 * pr * pi
    tab_ref[2], tab_ref[3] = pr, pi
    pr, pi = pr * pr - pi * pi, 2.0 * pr * pi
    tab_ref[4], tab_ref[5] = pr, pi
    pr, pi = pr * pr - pi * pi, 2.0 * pr * pi
    tab_ref[6], tab_ref[7] = pr, pi


def _s5_prep(a_re, a_im, log_dt, b_re, b_im, c_re, c_im):
    nl, g, p = a_re.shape
    h = b_re.shape[-1]
    rep = lambda v: jnp.repeat(v, h, axis=-1)
    ldt = jnp.broadcast_to(log_dt[:, :, None], (nl, g, p))
    nlb = g * p // LANES
    flat = lambda v: v.reshape(nl, nlb, LANES)
    wide = pl.BlockSpec((None, g, p * h), lambda l: (l, 0, 0))
    narrow = pl.BlockSpec((None, nlb, LANES), lambda l: (l, 0, 0))
    bbr, bbi, tab = pl.pallas_call(
        _s5_prep_kernel,
        grid=(nl,),
        in_specs=[wide] * 5 + [narrow] * 3,
        out_specs=(wide, wide, pl.BlockSpec((None, 8, nlb, LANES), lambda l: (l, 0, 0, 0))),
        out_shape=(jax.ShapeDtypeStruct((nl, g, p * h), F32),
                   jax.ShapeDtypeStruct((nl, g, p * h), F32),
                   jax.ShapeDtypeStruct((nl, 8, nlb, LANES), F32)),
        name="s5_prep",
    )(rep(a_re), rep(a_im), rep(ldt), b_re.reshape(nl, g, p * h), b_im.reshape(nl, g, p * h),
      flat(a_re), flat(a_im), flat(ldt))
    ncb = g // 8
    eye = jnp.eye(8, dtype=F32)
    bb = jnp.stack([bbr, bbi]).reshape(2, nl, ncb, 8, p, h)
    bblk = jnp.einsum('rlcgph,gk->lcrghkp', bb, eye).reshape(nl, ncb, 2, 8 * h, 8 * p)
    cc = jnp.stack([c_re, -c_im]).reshape(2, nl, ncb, 8, h, p)
    cblk = jnp.einsum('rlcghp,gk->lcrkpgh', cc, eye).reshape(nl, ncb, 2, 8 * p, 8 * h)
    tab = jnp.broadcast_to(tab[:, :, :, None, :], (nl, 8, nlb, SUBLANES, LANES))
    return bblk.astype(BF16), cblk.astype(BF16), tab


def _inproj_kernel(x_ref, g_ref, w_ref, u_ref, z_ref, xbc_ref, dt_ref, *, d_s5, d_ssd, d_conv):
    h = _rms(x_ref[...], g_ref[...]).astype(BF16)
    for c in range(d_s5 // LANES):
        u_ref[c] = _dot(h, w_ref[:, c * LANES:(c + 1) * LANES])
    o = d_s5
    z_ref[...] = _dot(h, w_ref[:, o:o + d_ssd])
    o += d_ssd
    xbc_ref[...] = _dot(h, w_ref[:, o:o + d_conv])
    o += d_conv
    dt_ref[...] = _dot(h, w_ref[:, o:o + LANES])


def _inproj(x, g, w, l, d_s5, d_ssd, d_conv):
    b, s, d = x.shape
    tm = TOK_TILE
    ncb = d_s5 // LANES
    return pl.pallas_call(
        functools.partial(_inproj_kernel, d_s5=d_s5, d_ssd=d_ssd, d_conv=d_conv),
        grid=(b, s // tm),
        in_specs=[pl.BlockSpec((None, tm, d), lambda i, j: (i, j, 0)),
                  _layer_spec(g, l), _layer_spec(w, l)],
        out_specs=(pl.BlockSpec((None, ncb, tm, LANES), lambda i, j: (i, 0, j, 0)),
                   pl.BlockSpec((None, tm, d_ssd), lambda i, j: (i, j, 0)),
                   pl.BlockSpec((None, tm, d_conv), lambda i, j: (i, j, 0)),
                   pl.BlockSpec((None, tm, LANES), lambda i, j: (i, j, 0))),
        out_shape=(jax.ShapeDtypeStruct((b, ncb, s, LANES), F32),
                   jax.ShapeDtypeStruct((b, s, d_ssd), F32),
                   jax.ShapeDtypeStruct((b, s, d_conv), F32),
                   jax.ShapeDtypeStruct((b, s, LANES), F32)),
        compiler_params=_cparams(("arbitrary", "arbitrary")),
        name="inproj",
    )(x, g, w)


def _s5_kernel(u_ref, bblk_ref, cblk_ref, tab_ref, dskip_ref, wglu_ref, y_ref,
               buf0, buf1, buf2, upad, ubf, yacc, ypad, carry):
    bufs = (buf0, buf1, buf2)
    nsl = 2 * S5_LB
    half = S5_SEG // 2

    @pl.when(pl.program_id(1) == 0)
    def _():
        carry[...] = jnp.zeros_like(carry)

    row = lax.broadcasted_iota(jnp.int32, (SUBLANES, LANES), 0)

    def shift_down(v, k):
        return jnp.where(row >= k, pltpu.roll(v, k, 0), 0.0)

    def cmul(ar, ai, xr, xi):
        return ar * xr - ai * xi, ar * xi + ai * xr

    def prep_u(c):
        for j in range(SUBLANES):
            upad[j * S5_PITCH:j * S5_PITCH + S5_SEG, :] = u_ref[c, j * S5_SEG:(j + 1) * S5_SEG, :]
        ubf[...] = jnp.concatenate(
            [upad[pl.ds(i, SUBLANES, stride=S5_PITCH), :] for i in range(S5_SEG)],
            axis=0).astype(BF16)

    def drive_half(c, it, dst):
        res = _dot(ubf[...], bblk_ref[c, it])
        for k in range(S5_LB):
            dst[S5_LB * it + k] = res[:, k * LANES:(k + 1) * LANES]

    def readout_half(c, it, src):
        lhs = jnp.concatenate([src[S5_LB * it + k] for k in range(S5_LB)], axis=1)
        yacc[...] += _dot(lhs.astype(BF16), cblk_ref[c, it])

    def unpermute(c):
        for i in range(S5_SEG):
            ypad[c, pl.ds(i, SUBLANES, stride=S5_PITCH), :] = yacc[i * SUBLANES:(i + 1) * SUBLANES, :]

    def scan_steps(cur, base, st, ar, ai, store):
        for s in range(half):
            r = pl.ds(pl.multiple_of((base + s) * SUBLANES, SUBLANES), SUBLANES)
            out = []
            for k in range(S5_LB):
                pr, pi = cmul(ar[k], ai[k], st[2 * k], st[2 * k + 1])
                nr, ni = pr + cur[k, r, :], pi + cur[S5_LB + k, r, :]
                if store:
                    cur[k, r, :] = nr
                    cur[S5_LB + k, r, :] = ni
                out += [nr, ni]
            st = tuple(out)
        return st

    prep_u(0)
    for it in range(2):
        drive_half(0, it, bufs[0])

    zero = jnp.zeros((SUBLANES, LANES), F32)
    for cb in range(S5_CB):
        cur, nxt, prv = bufs[cb % 3], bufs[(cb + 1) % 3], bufs[(cb + 2) % 3]
        lbs = [cb * S5_LB + k for k in range(S5_LB)]
        ar = [tab_ref[0, lb] for lb in lbs]
        ai = [tab_ref[1, lb] for lb in lbs]
        has_next, has_prev = cb + 1 < S5_CB, cb > 0

        if has_next:
            prep_u(cb + 1)

        def pass1(it, st, cur=cur, nxt=nxt, ar=ar, ai=ai, cb=cb, has_next=has_next):
            if has_next:
                drive_half(cb + 1, it, nxt)
            return scan_steps(cur, it * half, st, ar, ai, False)

        ends = (zero,) * nsl
        for it in range(2):
            ends = pass1(it, ends)

        init = []
        for k in range(S5_LB):
            lb = lbs[k]
            er, ei = ends[2 * k], ends[2 * k + 1]
            xr = jnp.where(row == 0, carry[0, lb], shift_down(er, 1))
            xi = jnp.where(row == 0, carry[1, lb], shift_down(ei, 1))
            for lvl, sh in ((2, 1), (4, 2), (6, 4)):
                pr, pi = cmul(tab_ref[lvl, lb], tab_ref[lvl + 1, lb],
                              shift_down(xr, sh), shift_down(xi, sh))
                xr, xi = xr + pr, xi + pi
            pr, pi = cmul(tab_ref[2, lb], tab_ref[3, lb], xr, xi)
            nr, ni = pr + er, pi + ei
            carry[0, lb] = jnp.broadcast_to(nr[SUBLANES - 1:SUBLANES, :], (SUBLANES, LANES))
            carry[1, lb] = jnp.broadcast_to(ni[SUBLANES - 1:SUBLANES, :], (SUBLANES, LANES))
            init += [xr, xi]

        if has_prev:
            yacc[...] = jnp.zeros_like(yacc)

        def pass2(it, st, cur=cur, prv=prv, ar=ar, ai=ai, cb=cb, has_prev=has_prev):
            if has_prev:
                readout_half(cb - 1, it, prv)
            return scan_steps(cur, it * half, st, ar, ai, True)

        st = tuple(init)
        for it in range(2):
            st = pass2(it, st)
        if has_prev:
            unpermute(cb - 1)

    last = S5_CB - 1
    yacc[...] = jnp.zeros_like(yacc)
    for it in range(2):
        readout_half(last, it, bufs[last % 3])
    unpermute(last)

    y = jnp.concatenate(
        [jnp.concatenate([ypad[c, j * S5_PITCH:j * S5_PITCH + S5_SEG, :]
                          for j in range(SUBLANES)], axis=0) + dskip_ref[c] * u_ref[c]
         for c in range(S5_CB)], axis=1)
    y = jax.nn.gelu(y, approximate=True)
    y_ref[...] = (y * _sigmoid(_dot(y.astype(BF16), wglu_ref[...]))).astype(BF16)


def _s5(u, bblk, cblk, tab, dskip, wglu, l):
    b, ncb, s, _ = u.shape
    d = ncb * LANES
    nlb = tab.shape[2]
    slab = pltpu.VMEM((2 * S5_LB, S5_TILE, LANES), F32)
    return pl.pallas_call(
        _s5_kernel,
        grid=(b, s // S5_TILE),
        in_specs=[pl.BlockSpec((None, ncb, S5_TILE, LANES), lambda i, j: (i, 0, j, 0)),
                  _layer_spec(bblk, l), _layer_spec(cblk, l), _layer_spec(tab, l),
                  _layer_spec(dskip, l), _layer_spec(wglu, l)],
        out_specs=pl.BlockSpec((None, S5_TILE, d), lambda i, j: (i, j, 0)),
        out_shape=jax.ShapeDtypeStruct((b, s, d), BF16),
        scratch_shapes=[slab, slab, slab,
                        pltpu.VMEM((S5_ROWS, LANES), F32),
                        pltpu.VMEM((S5_TILE, LANES), BF16),
                        pltpu.VMEM((S5_TILE, LANES), F32),
                        pltpu.VMEM((ncb, S5_ROWS, LANES), F32),
                        pltpu.VMEM((2, nlb, SUBLANES, LANES), F32)],
        compiler_params=_cparams(("arbitrary", "arbitrary")),
        name="s5_mixer",
    )(u, bblk, cblk, tab, dskip, wglu)


def _split3(v):
    hi = v.astype(BF16)
    r = v - hi.astype(F32)
    mid = r.astype(BF16)
    lo = (r - mid.astype(F32)).astype(BF16)
    return hi, mid, lo


def _ssd_kernel(z_ref, xbc_ref, dt_ref, cw_ref, cbias_ref, dtb_ref, alog_ref, alogc_ref,
                dexp_ref, ng_ref, y_ref, tail, state, *, d_ssd, nheads):
    L = SSD_CHUNK
    P = SSD_HEADDIM
    N = SSD_STATE

    @pl.when(pl.program_id(1) == 0)
    def _():
        tail[...] = jnp.zeros_like(tail)
        state[...] = jnp.zeros_like(state)

    x = xbc_ref[...]
    xfull = jnp.concatenate([tail[...], x], axis=0)
    tail[...] = x[L - SUBLANES:, :]
    conv = cbias_ref[...] + cw_ref[SSD_CONV - 1:SSD_CONV, :] * x
    for k in range(SSD_CONV - 1):
        sh = SSD_CONV - 1 - k
        conv = conv + cw_ref[k:k + 1, :] * pltpu.roll(xfull, sh, 0)[SUBLANES:, :]
    xa = conv * _sigmoid(conv)
    xs = xa[:, :d_ssd]
    bm = xa[:, d_ssd:d_ssd + SSD_GROUPS * N]
    cm = xa[:, d_ssd + SSD_GROUPS * N:]

    t = dt_ref[...] + dtb_ref[...]
    dt = jnp.maximum(t, 0.0) + jnp.log(1.0 + jnp.exp(-jnp.abs(t)))
    dt_t = dt.T
    a = dt * (-jnp.exp(alog_ref[...]))
    a_t = dt_t * (-jnp.exp(alogc_ref[...]))

    ri = lax.broadcasted_iota(jnp.int32, (L, L), 0)
    ci = lax.broadcasted_iota(jnp.int32, (L, L), 1)
    causal = ri >= ci
    ltri = causal.astype(BF16)
    utri = (ri <= ci).astype(BF16)
    acum = sum(_dot(ltri, part) for part in _split3(a))
    acum_t = sum(_dot(part, utri) for part in _split3(a_t))
    wdec_t = dt_t * jnp.exp(acum_t[:, L - 1:L] - acum_t)
    low = lax.broadcasted_iota(jnp.int32, (1, 2 * P), 1) < P

    ys = []
    for q in range(nheads // 2):
        g = (2 * q * SSD_GROUPS) // nheads
        if (2 * q * SSD_GROUPS) % nheads == 0:
            bg = bm[:, g * N:(g + 1) * N]
            cg = cm[:, g * N:(g + 1) * N]
            cb = _dot_nt(cg.astype(BF16), bg.astype(BF16))
            bg_t = bg.T
        xs_q = xs[:, q * 2 * P:(q + 1) * 2 * P].astype(BF16)
        st = state[q]
        rhs = jnp.concatenate([xs_q, st.astype(BF16)], axis=0)
        yq, upd, el = [], [], []
        for par in range(2):
            r = 2 * q + par
            col = jnp.broadcast_to(acum[:, r:r + 1], (L, L))
            decay = jnp.exp(jnp.where(causal, col - acum_t[r:r + 1, :], -1e30))
            m = cb * decay * dt_t[r:r + 1, :]
            e_col = jnp.exp(col)
            lhs = jnp.concatenate([m, cg * e_col], axis=1).astype(BF16)
            yq.append(_dot(lhs, rhs))
            upd.append(_dot((bg_t * wdec_t[r:r + 1, :]).astype(BF16), xs_q))
            el.append(e_col[L - 1:L, :])
        ys.append(jnp.where(low, yq[0], yq[1]))
        state[q] = st * jnp.where(low, el[0], el[1]) + jnp.where(low, upd[0], upd[1])
    y = jnp.concatenate(ys, axis=1) + dexp_ref[...] * xs
    z = z_ref[...]
    y_ref[...] = _rms(y * (z * _sigmoid(z)), ng_ref[...]).astype(BF16)


def _ssd(z, xbc, dtp, conv_w, conv_b, dt_bias, a_log, a_log_col, dexp, norm_g, l):
    b, s, d_ssd = z.shape
    d_conv = xbc.shape[-1]
    nheads = d_ssd // SSD_HEADDIM
    L = SSD_CHUNK
    params = (conv_w, conv_b, dt_bias, a_log, a_log_col, dexp, norm_g)
    return pl.pallas_call(
        functools.partial(_ssd_kernel, d_ssd=d_ssd, nheads=nheads),
        grid=(b, s // L),
        in_specs=[pl.BlockSpec((None, L, d_ssd), lambda i, j: (i, j, 0)),
                  pl.BlockSpec((None, L, d_conv), lambda i, j: (i, j, 0)),
                  pl.BlockSpec((None, L, LANES), lambda i, j: (i, j, 0))]
                 + [_layer_spec(p, l) for p in params],
        out_specs=pl.BlockSpec((None, L, d_ssd), lambda i, j: (i, j, 0)),
        out_shape=jax.ShapeDtypeStruct((b, s, d_ssd), BF16),
        scratch_shapes=[pltpu.VMEM((SUBLANES, d_conv), F32),
                        pltpu.VMEM((nheads // 2, SSD_STATE, 2 * SSD_HEADDIM), F32)],
        compiler_params=_cparams(("arbitrary", "arbitrary")),
        name="ssd_mixer",
    )(z, xbc, dtp, *params)


def _kv_kernel(mem_ref, g_ref, wk_ref, wv_ref, k_ref, v_ref):
    m = _rms(mem_ref[...], g_ref[...]).astype(BF16)
    k_ref[...] = _dot(m, wk_ref[...]).astype(BF16)
    v_ref[...] = _dot(m, wv_ref[...]).astype(BF16)


def _kv(mem, norm_mem, wk, wv):
    b, m, d = mem.shape
    nl = wk.shape[0]
    out = jax.ShapeDtypeStruct((nl, b, m, d), BF16)
    return pl.pallas_call(
        _kv_kernel,
        grid=(nl, b),
        in_specs=[pl.BlockSpec((None, m, d), lambda l, i: (i, 0, 0)),
                  pl.BlockSpec((None, 1, d), lambda l, i: (l, 0, 0)),
                  pl.BlockSpec((None, d, d), lambda l, i: (l, 0, 0)),
                  pl.BlockSpec((None, d, d), lambda l, i: (l, 0, 0))],
        out_specs=(pl.BlockSpec((None, None, m, d), lambda l, i: (l, i, 0, 0)),
                   pl.BlockSpec((None, None, m, d), lambda l, i: (l, i, 0, 0))),
        out_shape=(out, out),
        compiler_params=_cparams(("arbitrary", "arbitrary")),
        name="mem_kv",
    )(mem, norm_mem, wk, wv)


def _attn_kernel(x_ref, y1_ref, y2_ref, wout_ref, g_ref, wq_ref, wo_ref, k_ref, v_ref, o_ref):
    d1 = y1_ref.shape[-1]
    x = x_ref[...] + _dot(y1_ref[...], wout_ref[:d1, :]) + _dot(y2_ref[...], wout_ref[d1:, :])
    d = x.shape[-1]
    hd = d // XA_HEADS
    h = _rms(x, g_ref[...]).astype(BF16)
    q = (_dot(h, wq_ref[...]) * (hd ** -0.5)).astype(BF16)
    outs = []
    for a in range(XA_HEADS):
        sl = slice(a * hd, (a + 1) * hd)
        s = _dot_nt(q[:, sl], k_ref[:, sl])
        e = jnp.exp(s - jnp.max(s, axis=-1, keepdims=True))
        p = e / jnp.sum(e, axis=-1, keepdims=True)
        outs.append(_dot(p.astype(BF16), v_ref[:, sl]))
    o = jnp.concatenate(outs, axis=1).astype(BF16)
    o_ref[...] = x + _dot(o, wo_ref[...])


def _attn(x, y1, y2, wout, g, wq, wo, k, v, l):
    b, s, d = x.shape
    m = k.shape[2]
    tm = TOK_TILE
    tile = lambda n: pl.BlockSpec((None, tm, n), lambda i, j: (i, j, 0))
    kv = pl.BlockSpec((None, None, m, d), lambda i, j: (l, i, 0, 0))
    return pl.pallas_call(
        _attn_kernel,
        grid=(b, s // tm),
        in_specs=[tile(d), tile(y1.shape[-1]), tile(y2.shape[-1]), _layer_spec(wout, l),
                  _layer_spec(g, l), _layer_spec(wq, l), _layer_spec(wo, l), kv, kv],
        out_specs=tile(d),
        out_shape=jax.ShapeDtypeStruct(x.shape, F32),
        compiler_params=_cparams(("arbitrary", "arbitrary")),
        name="outproj_xattn",
    )(x, y1, y2, wout, g, wq, wo, k, v)


def _mlp_kernel(x_ref, g_ref, w1_ref, w2_ref, gf_ref, o_ref, *, final_norm):
    x = x_ref[...]
    d = x.shape[-1]
    h = _rms(x, g_ref[...]).astype(BF16)
    acc = x
    for c in range(w1_ref.shape[1] // d):
        a = jnp.maximum(_dot(h, w1_ref[:, c * d:(c + 1) * d]), 0.0)
        acc = acc + _dot((a * a).astype(BF16), w2_ref[c * d:(c + 1) * d, :])
    o_ref[...] = _rms(acc, gf_ref[...]) if final_norm else acc


def _mlp(x, g, w1, w2, gf, l, final_norm):
    b, s, d = x.shape
    tm = TOK_TILE
    tile = pl.BlockSpec((None, tm, d), lambda i, j: (i, j, 0))
    vec = pl.BlockSpec((1, d), lambda i, j: (0, 0))
    return pl.pallas_call(
        functools.partial(_mlp_kernel, final_norm=final_norm),
        grid=(b, s // tm),
        in_specs=[tile, _layer_spec(g, l), _layer_spec(w1, l), _layer_spec(w2, l), vec],
        out_specs=tile,
        out_shape=jax.ShapeDtypeStruct(x.shape, F32),
        compiler_params=_cparams(("arbitrary", "arbitrary")),
        name="mlp",
    )(x, g, w1, w2, gf)


def kernel(x, mem, norm_mix, w_in, s5_a_re, s5_a_im, s5_log_dt, s5_b_re, s5_b_im, s5_c_re, s5_c_im, s5_d, s5_w_glu, ssd_conv_w, ssd_conv_b, ssd_dt_bias, ssd_a_log, ssd_d, ssd_norm, w_out, norm_xattn, norm_mem, xa_wq, xa_wk, xa_wv, xa_wo, norm_mlp, mlp_w1, mlp_w2, norm_final):
    depth = w_in.shape[0]
    d_s5 = s5_w_glu.shape[-1]
    d_ssd = ssd_norm.shape[-1]
    d_conv = ssd_conv_w.shape[-1]
    nheads = ssd_dt_bias.shape[-1]
    rows = lambda v: v[:, None, :]
    padl = lambda v: jnp.pad(v, ((0, 0), (0, LANES - v.shape[-1])))
    bf = lambda v: v.astype(BF16)

    w_in_p = bf(jnp.pad(w_in, ((0, 0), (0, 0), (0, LANES - nheads))))
    bblk, cblk, tab = _s5_prep(s5_a_re, s5_a_im, s5_log_dt, s5_b_re, s5_b_im, s5_c_re, s5_c_im)
    dskip = s5_d.reshape(depth, -1, 1, LANES)
    a_log = padl(ssd_a_log)
    a_log_col = jnp.broadcast_to(a_log[:, :, None], (depth, LANES, LANES))
    dexp = jnp.repeat(ssd_d, SSD_HEADDIM, axis=-1)
    w_glu, w_o, wq, wo, w1, w2 = (bf(s5_w_glu), bf(w_out), bf(xa_wq), bf(xa_wo),
                                  bf(mlp_w1), bf(mlp_w2))
    k_all, v_all = _kv(mem, rows(norm_mem), bf(xa_wk), bf(xa_wv))

    for l in range(depth):
        u, z, xbc, dtp = _inproj(x, rows(norm_mix), w_in_p, l, d_s5, d_ssd, d_conv)
        y_s5 = _s5(u, bblk, cblk, tab, dskip, w_glu, l)
        y_ssd = _ssd(z, xbc, dtp, ssd_conv_w, rows(ssd_conv_b), rows(padl(ssd_dt_bias)),
                     rows(a_log), a_log_col, rows(dexp), rows(ssd_norm), l)
        x = _attn(x, y_s5, y_ssd, w_o, rows(norm_xattn), wq, wo, k_all, v_all, l)
        x = _mlp(x, rows(norm_mlp), w1, w2, norm_final.reshape(1, -1), l, l == depth - 1)
    return x
```

```python
import functools
import math

import jax
import jax.numpy as jnp
from jax import lax
from jax.experimental import pallas as pl
from jax.experimental.pallas import tpu as pltpu

F32 = jnp.float32
BF16 = jnp.bfloat16
EPS = 1e-5

LANES = 128
SUBLANES = 8

S5_GROUP = 16
S5_STATE = 64
SSD_HEADDIM = 64
SSD_GROUPS = 4
SSD_STATE = 128
SSD_CONV = 4
SSD_CHUNK = 128
XA_HEADS = 4

S5_TILE = 512
S5_SEG = S5_TILE // SUBLANES
S5_PITCH = S5_SEG + SUBLANES
S5_ROWS = SUBLANES * S5_PITCH
S5_CB = 8
S5_LB = 4

TOK_TILE = 512
MIX_HALF = 256
PROJ_COLS = 256
CONV_COLS = 512
VMEM_LIMIT = 56 * 1024 * 1024


def _layer_spec(arr, l):
    nd = arr.ndim - 1
    return pl.BlockSpec((None,) + arr.shape[1:], lambda *_: (l,) + (0,) * nd)


def _cparams(sem):
    return pltpu.CompilerParams(dimension_semantics=sem, vmem_limit_bytes=VMEM_LIMIT)


def _rms(x, g):
    return x * lax.rsqrt(jnp.mean(x * x, axis=-1, keepdims=True) + EPS) * g


def _dot(a, b):
    return jnp.dot(a, b, preferred_element_type=F32)


def _dot_nt(a, b):
    return lax.dot_general(a, b, (((1,), (1,)), ((), ())), preferred_element_type=F32)


def _sigmoid(x):
    return 1.0 / (1.0 + jnp.exp(-x))


def _s5_prep_kernel(arb_ref, aib_ref, ldb_ref, br_ref, bi_ref, ar_ref, ai_ref, ld_ref,
                    bbr_ref, bbi_ref, tab_ref):
    def disc(ar, ai, ld):
        dt = jnp.exp(ld)
        mag = jnp.exp(dt * ar)
        return mag * jnp.cos(dt * ai), mag * jnp.sin(dt * ai)

    ar, ai = arb_ref[...], aib_ref[...]
    abr, abi = disc(ar, ai, ldb_ref[...])
    den = ar * ar + ai * ai
    zr, zi = abr - 1.0, abi
    fr = (zr * ar + zi * ai) / den
    fi = (zi * ar - zr * ai) / den
    br, bi = br_ref[...], bi_ref[...]
    bbr_ref[...] = fr * br - fi * bi
    bbi_ref[...] = fr * bi + fi * br

    pr, pi = disc(ar_ref[...], ai_ref[...], ld_ref[...])
    tab_ref[0], tab_ref[1] = pr, pi
    for _ in range(int(math.log2(S5_SEG))):
        pr, pi = pr * pr - pi * pi, 2.0 * pr * pi
    tab_ref[2], tab_ref[3] = pr, pi
    pr, pi = pr * pr - pi * pi, 2.0 * pr * pi
    tab_ref[4], tab_ref[5] = pr, pi
    pr, pi = pr * pr - pi * pi, 2.0 * pr * pi
    tab_ref[6], tab_ref[7] = pr, pi


def _s5_prep(a_re, a_im, log_dt, b_re, b_im, c_re, c_im):
    nl, g, p = a_re.shape
    h = b_re.shape[-1]
    rep = lambda v: jnp.repeat(v, h, axis=-1)
    ldt = jnp.broadcast_to(log_dt[:, :, None], (nl, g, p))
    nlb = g * p // LANES
    flat = lambda v: v.reshape(nl, nlb, LANES)
    wide = pl.BlockSpec((None, g, p * h), lambda l: (l, 0, 0))
    narrow = pl.BlockSpec((None, nlb, LANES), lambda l: (l, 0, 0))
    bbr, bbi, tab = pl.pallas_call(
        _s5_prep_kernel,
        grid=(nl,),
        in_specs=[wide] * 5 + [narrow] * 3,
        out_specs=(wide, wide, pl.BlockSpec((None, 8, nlb, LANES), lambda l: (l, 0, 0, 0))),
        out_shape=(jax.ShapeDtypeStruct((nl, g, p * h), F32),
                   jax.ShapeDtypeStruct((nl, g, p * h), F32),
                   jax.ShapeDtypeStruct((nl, 8, nlb, LANES), F32)),
        name="s5_prep",
    )(rep(a_re), rep(a_im), rep(ldt), b_re.reshape(nl, g, p * h), b_im.reshape(nl, g, p * h),
      flat(a_re), flat(a_im), flat(ldt))
    ncb = g // 8
    eye = jnp.eye(8, dtype=F32)
    bb = jnp.stack([bbr, bbi]).reshape(2, nl, ncb, 8, p, h)
    bblk = jnp.einsum('rlcgph,gk->lcrghkp', bb, eye).reshape(nl, ncb, 2, 8 * h, 8 * p)
    cc = jnp.stack([c_re, -c_im]).reshape(2, nl, ncb, 8, h, p)
    cblk = jnp.einsum('rlcghp,gk->lcrkpgh', cc, eye).reshape(nl, ncb, 2, 8 * p, 8 * h)
    tab = jnp.broadcast_to(tab[:, :, :, None, :], (nl, 8, nlb, SUBLANES, LANES))
    return bblk.astype(BF16), cblk.astype(BF16), tab


def _split3(v):
    hi = v.astype(BF16)
    r = v - hi.astype(F32)
    mid = r.astype(BF16)
    lo = (r - mid.astype(F32)).astype(BF16)
    return hi, mid, lo


def _mix_kernel(x_ref, g_ref, w_ref, cw_ref, cbias_ref, dtb_ref, alog_ref, alogc_ref, dexp_ref,
                ng_ref, u_ref, ya_ref, yb_ref, za, xa, da, zb, xb, db, hbuf, tail, state,
                *, d_s5, d_ssd, d_conv, nheads, nt):
    L = SSD_CHUNK
    P = SSD_HEADDIM
    N = SSD_STATE
    j = pl.program_id(1)

    def reset_carry():
        tail[...] = jnp.zeros_like(tail)
        state[...] = jnp.zeros_like(state)

    @pl.when(j == 0)
    def _():
        zb[...] = jnp.zeros_like(zb)
        xb[...] = jnp.zeros_like(xb)
        db[...] = jnp.zeros_like(db)
        reset_carry()

    def project_pieces(rows, z_s, x_s, d_s):
        hbuf[...] = _rms(x_ref[rows, :], g_ref[...]).astype(BF16)

        def u_store(c0):
            def f(r):
                for k in range(r.shape[1] // LANES):
                    u_ref[c0 // LANES + k, rows, :] = r[:, k * LANES:(k + 1) * LANES]
            return f

        def col_store(ref, c0):
            def f(r):
                ref[:, c0:c0 + r.shape[1]] = r
            return f

        def piece(w0, width, store):
            return lambda: store(_dot(hbuf[...], w_ref[:, w0:w0 + width]))

        pieces = [piece(c0, PROJ_COLS, u_store(c0)) for c0 in range(0, d_s5, PROJ_COLS)]
        o = d_s5
        pieces += [piece(o + c0, PROJ_COLS, col_store(z_s, c0)) for c0 in range(0, d_ssd, PROJ_COLS)]
        o += d_ssd
        pieces += [piece(o + c0, PROJ_COLS, col_store(x_s, c0)) for c0 in range(0, d_conv, PROJ_COLS)]
        o += d_conv
        pieces.append(piece(o, LANES, col_store(d_s, 0)))
        return pieces

    def emitter(pieces, nslots):
        done = [0, 0]

        def issue_upto(n):
            while done[1] < min(n, len(pieces)):
                pieces[done[1]]()
                done[1] += 1

        def emit():
            done[0] += 1
            issue_upto(-(-done[0] * len(pieces) // nslots))

        return emit, lambda: issue_upto(len(pieces))

    ri = lax.broadcasted_iota(jnp.int32, (L, L), 0)
    ci = lax.broadcasted_iota(jnp.int32, (L, L), 1)
    causal = ri >= ci
    ltri = causal.astype(BF16)
    utri = (ri <= ci).astype(BF16)
    low = lax.broadcasted_iota(jnp.int32, (1, 2 * P), 1) < P

    def ssd_chunk(z, x_s, rows, dt_raw, emit):
        blocks = []
        for c0 in range(0, d_conv, CONV_COLS):
            cols = slice(c0, c0 + CONV_COLS)
            x = x_s[rows, cols]
            xfull = jnp.concatenate([tail[:, cols], x], axis=0)
            tail[:, cols] = x[L - SUBLANES:, :]
            conv = cbias_ref[:, cols] + cw_ref[SSD_CONV - 1:SSD_CONV, cols] * x
            for k in range(SSD_CONV - 1):
                sh = SSD_CONV - 1 - k
                conv = conv + cw_ref[k:k + 1, cols] * pltpu.roll(xfull, sh, 0)[SUBLANES:, :]
            blocks.append(conv * _sigmoid(conv))
            emit()
        act = jnp.concatenate(blocks, axis=1)
        xs = act[:, :d_ssd]
        bm = act[:, d_ssd:d_ssd + SSD_GROUPS * N]
        cm = act[:, d_ssd + SSD_GROUPS * N:]

        t = dt_raw + dtb_ref[...]
        dt = jnp.maximum(t, 0.0) + jnp.log(1.0 + jnp.exp(-jnp.abs(t)))
        dt_t = dt.T
        a = dt * (-jnp.exp(alog_ref[...]))
        a_t = dt_t * (-jnp.exp(alogc_ref[...]))
        acum = sum(_dot(ltri, part) for part in _split3(a))
        acum_t = sum(_dot(part, utri) for part in _split3(a_t))
        wdec_t = dt_t * jnp.exp(acum_t[:, L - 1:L] - acum_t)
        emit()

        ys = []
        for q in range(nheads // 2):
            g = (2 * q * SSD_GROUPS) // nheads
            if (2 * q * SSD_GROUPS) % nheads == 0:
                bg = bm[:, g * N:(g + 1) * N]
                cg = cm[:, g * N:(g + 1) * N]
                cb = _dot_nt(cg.astype(BF16), bg.astype(BF16))
                bg_t = bg.T
            xs_q = xs[:, q * 2 * P:(q + 1) * 2 * P].astype(BF16)
            st = state[q]
            rhs = jnp.concatenate([xs_q, st.astype(BF16)], axis=0)
            yq, upd, el = [], [], []
            for par in range(2):
                r = 2 * q + par
                col = jnp.broadcast_to(acum[:, r:r + 1], (L, L))
                decay = jnp.exp(jnp.where(causal, col - acum_t[r:r + 1, :], -1e30))
                m = cb * decay * dt_t[r:r + 1, :]
                e_col = jnp.exp(col)
                lhs = jnp.concatenate([m, cg * e_col], axis=1).astype(BF16)
                yq.append(_dot(lhs, rhs))
                upd.append(_dot((bg_t * wdec_t[r:r + 1, :]).astype(BF16), xs_q))
                el.append(e_col[L - 1:L, :])
            ys.append(jnp.where(low, yq[0], yq[1]))
            state[q] = st * jnp.where(low, el[0], el[1]) + jnp.where(low, upd[0], upd[1])
            emit()
        y = jnp.concatenate(ys, axis=1) + dexp_ref[...] * xs
        return _rms(y * (z * _sigmoid(z)), ng_ref[...]).astype(BF16)

    def phase(proj_rows, p_scratch, s_scratch, y_ref):
        slots_per_chunk = d_conv // CONV_COLS + 1 + nheads // 2
        emit, flush = emitter(project_pieces(proj_rows, *p_scratch),
                              (MIX_HALF // L) * slots_per_chunk)
        z_s, x_s, d_s = s_scratch
        for c in range(MIX_HALF // L):
            r = slice(c * L, (c + 1) * L)
            y_ref[r, :] = ssd_chunk(z_s[r, :], x_s, r, d_s[r, :], emit)
        flush()

    phase(slice(0, MIX_HALF), (za, xa, da), (zb, xb, db), yb_ref)

    @pl.when(j == 0)
    def _():
        reset_carry()

    @pl.when(j < nt)
    def _():
        phase(slice(MIX_HALF, 2 * MIX_HALF), (zb, xb, db), (za, xa, da), ya_ref)


def _mix(x, g, w, conv_w, conv_b, dt_bias, a_log, a_log_col, dexp, norm_g, l, d_s5, d_ssd, d_conv):
    b, s, d = x.shape
    tm = 2 * MIX_HALF
    nt = s // tm
    ncb = d_s5 // LANES
    nheads = d_ssd // SSD_HEADDIM
    params = (g, w, conv_w, conv_b, dt_bias, a_log, a_log_col, dexp, norm_g)
    cur = lambda i, j: jnp.minimum(j, nt - 1)
    prev = lambda i, j: jnp.maximum(j - 1, 0)
    half = lambda n: pltpu.VMEM((MIX_HALF, n), F32)
    yshape = jax.ShapeDtypeStruct((b, nt, MIX_HALF, d_ssd), BF16)
    return pl.pallas_call(
        functools.partial(_mix_kernel, d_s5=d_s5, d_ssd=d_ssd, d_conv=d_conv, nheads=nheads, nt=nt),
        grid=(b, nt + 1),
        in_specs=[pl.BlockSpec((None, tm, d), lambda i, j: (i, cur(i, j), 0))]
                 + [_layer_spec(p, l) for p in params],
        out_specs=(pl.BlockSpec((None, ncb, tm, LANES), lambda i, j: (i, 0, cur(i, j), 0)),
                   pl.BlockSpec((None, None, MIX_HALF, d_ssd), lambda i, j: (i, cur(i, j), 0, 0)),
                   pl.BlockSpec((None, None, MIX_HALF, d_ssd), lambda i, j: (i, prev(i, j), 0, 0))),
        out_shape=(jax.ShapeDtypeStruct((b, ncb, s, LANES), F32), yshape, yshape),
        scratch_shapes=[half(d_ssd), half(d_conv), half(LANES),
                        half(d_ssd), half(d_conv), half(LANES),
                        pltpu.VMEM((MIX_HALF, d), BF16),
                        pltpu.VMEM((SUBLANES, d_conv), F32),
                        pltpu.VMEM((nheads // 2, SSD_STATE, 2 * SSD_HEADDIM), F32)],
        compiler_params=_cparams(("arbitrary", "arbitrary")),
        name="inproj_ssd",
    )(x, *params)


def _s5_kernel(u_ref, bblk_ref, cblk_ref, tab_ref, dskip_ref, wglu_ref, y_ref,
               buf0, buf1, buf2, upad, ubf, yacc, ypad, carry):
    bufs = (buf0, buf1, buf2)
    nsl = 2 * S5_LB
    half = S5_SEG // 2

    @pl.when(pl.program_id(1) == 0)
    def _():
        carry[...] = jnp.zeros_like(carry)

    row = lax.broadcasted_iota(jnp.int32, (SUBLANES, LANES), 0)

    def shift_down(v, k):
        return jnp.where(row >= k, pltpu.roll(v, k, 0), 0.0)

    def cmul(ar, ai, xr, xi):
        return ar * xr - ai * xi, ar * xi + ai * xr

    def prep_u(c):
        for j in range(SUBLANES):
            upad[j * S5_PITCH:j * S5_PITCH + S5_SEG, :] = u_ref[c, j * S5_SEG:(j + 1) * S5_SEG, :]
        ubf[...] = jnp.concatenate(
            [upad[pl.ds(i, SUBLANES, stride=S5_PITCH), :] for i in range(S5_SEG)],
            axis=0).astype(BF16)

    def drive_half(c, it, dst):
        res = _dot(ubf[...], bblk_ref[c, it])
        for k in range(S5_LB):
            dst[S5_LB * it + k] = res[:, k * LANES:(k + 1) * LANES]

    def readout_half(c, it, src):
        lhs = jnp.concatenate([src[S5_LB * it + k] for k in range(S5_LB)], axis=1)
        yacc[...] += _dot(lhs.astype(BF16), cblk_ref[c, it])

    def unpermute(c):
        for i in range(S5_SEG):
            ypad[c, pl.ds(i, SUBLANES, stride=S5_PITCH), :] = yacc[i * SUBLANES:(i + 1) * SUBLANES, :]

    def scan_steps(cur, base, st, ar, ai, store):
        for s in range(half):
            r = pl.ds(pl.multiple_of((base + s) * SUBLANES, SUBLANES), SUBLANES)
            out = []
            for k in range(S5_LB):
                pr, pi = cmul(ar[k], ai[k], st[2 * k], st[2 * k + 1])
                nr, ni = pr + cur[k, r, :], pi + cur[S5_LB + k, r, :]
                if store:
                    cur[k, r, :] = nr
                    cur[S5_LB + k, r, :] = ni
                out += [nr, ni]
            st = tuple(out)
        return st

    prep_u(0)
    for it in range(2):
        drive_half(0, it, bufs[0])

    zero = jnp.zeros((SUBLANES, LANES), F32)
    for cb in range(S5_CB):
        cur, nxt, prv = bufs[cb % 3], bufs[(cb + 1) % 3], bufs[(cb + 2) % 3]
        lbs = [cb * S5_LB + k for k in range(S5_LB)]
        ar = [tab_ref[0, lb] for lb in lbs]
        ai = [tab_ref[1, lb] for lb in lbs]
        has_next, has_prev = cb + 1 < S5_CB, cb > 0

        if has_next:
            prep_u(cb + 1)

        def pass1(it, st, cur=cur, nxt=nxt, ar=ar, ai=ai, cb=cb, has_next=has_next):
            if has_next:
                drive_half(cb + 1, it, nxt)
            return scan_steps(cur, it * half, st, ar, ai, False)

        ends = (zero,) * nsl
        for it in range(2):
            ends = pass1(it, ends)

        init = []
        for k in range(S5_LB):
            lb = lbs[k]
            er, ei = ends[2 * k], ends[2 * k + 1]
            xr = jnp.where(row == 0, carry[0, lb], shift_down(er, 1))
            xi = jnp.where(row == 0, carry[1, lb], shift_down(ei, 1))
            for lvl, sh in ((2, 1), (4, 2), (6, 4)):
                pr, pi = cmul(tab_ref[lvl, lb], tab_ref[lvl + 1, lb],
                              shift_down(xr, sh), shift_down(xi, sh))
                xr, xi = xr + pr, xi + pi
            pr, pi = cmul(tab_ref[2, lb], tab_ref[3, lb], xr, xi)
            nr, ni = pr + er, pi + ei
            carry[0, lb] = jnp.broadcast_to(nr[SUBLANES - 1:SUBLANES, :], (SUBLANES, LANES))
            carry[1, lb] = jnp.broadcast_to(ni[SUBLANES - 1:SUBLANES, :], (SUBLANES, LANES))
            init += [xr, xi]

        if has_prev:
            yacc[...] = jnp.zeros_like(yacc)

        def pass2(it, st, cur=cur, prv=prv, ar=ar, ai=ai, cb=cb, has_prev=has_prev):
            if has_prev:
                readout_half(cb - 1, it, prv)
            return scan_steps(cur, it * half, st, ar, ai, True)

        st = tuple(init)
        for it in range(2):
            st = pass2(it, st)
        if has_prev:
            unpermute(cb - 1)

    last = S5_CB - 1
    yacc[...] = jnp.zeros_like(yacc)
    for it in range(2):
        readout_half(last, it, bufs[last % 3])
    unpermute(last)

    y = jnp.concatenate(
        [jnp.concatenate([ypad[c, j * S5_PITCH:j * S5_PITCH + S5_SEG, :]
                          for j in range(SUBLANES)], axis=0) + dskip_ref[c] * u_ref[c]
         for c in range(S5_CB)], axis=1)
    y = jax.nn.gelu(y, approximate=True)
    y_ref[...] = (y * _sigmoid(_dot(y.astype(BF16), wglu_ref[...]))).astype(BF16)


def _s5(u, bblk, cblk, tab, dskip, wglu, l):
    b, ncb, s, _ = u.shape
    d = ncb * LANES
    nlb = tab.shape[2]
    slab = pltpu.VMEM((2 * S5_LB, S5_TILE, LANES), F32)
    return pl.pallas_call(
        _s5_kernel,
        grid=(b, s // S5_TILE),
        in_specs=[pl.BlockSpec((None, ncb, S5_TILE, LANES), lambda i, j: (i, 0, j, 0)),
                  _layer_spec(bblk, l), _layer_spec(cblk, l), _layer_spec(tab, l),
                  _layer_spec(dskip, l), _layer_spec(wglu, l)],
        out_specs=pl.BlockSpec((None, S5_TILE, d), lambda i, j: (i, j, 0)),
        out_shape=jax.ShapeDtypeStruct((b, s, d), BF16),
        scratch_shapes=[slab, slab, slab,
                        pltpu.VMEM((S5_ROWS, LANES), F32),
                        pltpu.VMEM((S5_TILE, LANES), BF16),
                        pltpu.VMEM((S5_TILE, LANES), F32),
                        pltpu.VMEM((ncb, S5_ROWS, LANES), F32),
                        pltpu.VMEM((2, nlb, SUBLANES, LANES), F32)],
        compiler_params=_cparams(("arbitrary", "arbitrary")),
        name="s5_mixer",
    )(u, bblk, cblk, tab, dskip, wglu)


def _kv_kernel(mem_ref, g_ref, wk_ref, wv_ref, k_ref, v_ref):
    m = _rms(mem_ref[...], g_ref[...]).astype(BF16)
    k_ref[...] = _dot(m, wk_ref[...]).astype(BF16)
    v_ref[...] = _dot(m, wv_ref[...]).astype(BF16)


def _kv(mem, norm_mem, wk, wv):
    b, m, d = mem.shape
    nl = wk.shape[0]
    out = jax.ShapeDtypeStruct((nl, b, m, d), BF16)
    return pl.pallas_call(
        _kv_kernel,
        grid=(nl, b),
        in_specs=[pl.BlockSpec((None, m, d), lambda l, i: (i, 0, 0)),
                  pl.BlockSpec((None, 1, d), lambda l, i: (l, 0, 0)),
                  pl.BlockSpec((None, d, d), lambda l, i: (l, 0, 0)),
                  pl.BlockSpec((None, d, d), lambda l, i: (l, 0, 0))],
        out_specs=(pl.BlockSpec((None, None, m, d), lambda l, i: (l, i, 0, 0)),
                   pl.BlockSpec((None, None, m, d), lambda l, i: (l, i, 0, 0))),
        out_shape=(out, out),
        compiler_params=_cparams(("arbitrary", "arbitrary")),
        name="mem_kv",
    )(mem, norm_mem, wk, wv)


def _attn_kernel(x_ref, y1_ref, y2a_ref, y2b_ref, wout_ref, g_ref, wq_ref, wo_ref, k_ref, v_ref,
                 o_ref):
    d1 = y1_ref.shape[-1]
    y2 = jnp.concatenate([y2a_ref[...], y2b_ref[...]], axis=0)
    x = x_ref[...] + _dot(y1_ref[...], wout_ref[:d1, :]) + _dot(y2, wout_ref[d1:, :])
    d = x.shape[-1]
    hd = d // XA_HEADS
    h = _rms(x, g_ref[...]).astype(BF16)
    q = (_dot(h, wq_ref[...]) * (hd ** -0.5)).astype(BF16)
    outs = []
    for a in range(XA_HEADS):
        sl = slice(a * hd, (a + 1) * hd)
        s = _dot_nt(q[:, sl], k_ref[:, sl])
        e = jnp.exp(s - jnp.max(s, axis=-1, keepdims=True))
        p = e / jnp.sum(e, axis=-1, keepdims=True)
        outs.append(_dot(p.astype(BF16), v_ref[:, sl]))
    o = jnp.concatenate(outs, axis=1).astype(BF16)
    o_ref[...] = x + _dot(o, wo_ref[...])


def _attn(x, y1, y2a, y2b, wout, g, wq, wo, k, v, l):
    b, s, d = x.shape
    m = k.shape[2]
    tm = 2 * MIX_HALF
    halfspec = pl.BlockSpec((None, None, MIX_HALF, y2a.shape[-1]), lambda i, j: (i, j, 0, 0))
    tile = lambda n: pl.BlockSpec((None, tm, n), lambda i, j: (i, j, 0))
    kv = pl.BlockSpec((None, None, m, d), lambda i, j: (l, i, 0, 0))
    return pl.pallas_call(
        _attn_kernel,
        grid=(b, s // tm),
        in_specs=[tile(d), tile(y1.shape[-1]), halfspec, halfspec, _layer_spec(wout, l),
                  _layer_spec(g, l), _layer_spec(wq, l), _layer_spec(wo, l), kv, kv],
        out_specs=tile(d),
        out_shape=jax.ShapeDtypeStruct(x.shape, F32),
        compiler_params=_cparams(("arbitrary", "arbitrary")),
        name="outproj_xattn",
    )(x, y1, y2a, y2b, wout, g, wq, wo, k, v)


def _mlp_kernel(x_ref, g_ref, w1_ref, w2_ref, gf_ref, o_ref, *, final_norm):
    x = x_ref[...]
    d = x.shape[-1]
    h = _rms(x, g_ref[...]).astype(BF16)
    acc = x
    for c in range(w1_ref.shape[1] // d):
        a = jnp.maximum(_dot(h, w1_ref[:, c * d:(c + 1) * d]), 0.0)
        acc = acc + _dot((a * a).astype(BF16), w2_ref[c * d:(c + 1) * d, :])
    o_ref[...] = _rms(acc, gf_ref[...]) if final_norm else acc


def _mlp(x, g, w1, w2, gf, l, final_norm):
    b, s, d = x.shape
    tm = TOK_TILE
    tile = pl.BlockSpec((None, tm, d), lambda i, j: (i, j, 0))
    vec = pl.BlockSpec((1, d), lambda i, j: (0, 0))
    return pl.pallas_call(
        functools.partial(_mlp_kernel, final_norm=final_norm),
        grid=(b, s // tm),
        in_specs=[tile, _layer_spec(g, l), _layer_spec(w1, l), _layer_spec(w2, l), vec],
        out_specs=tile,
        out_shape=jax.ShapeDtypeStruct(x.shape, F32),
        compiler_params=_cparams(("arbitrary", "arbitrary")),
        name="mlp",
    )(x, g, w1, w2, gf)


def kernel(x, mem, norm_mix, w_in, s5_a_re, s5_a_im, s5_log_dt, s5_b_re, s5_b_im, s5_c_re, s5_c_im, s5_d, s5_w_glu, ssd_conv_w, ssd_conv_b, ssd_dt_bias, ssd_a_log, ssd_d, ssd_norm, w_out, norm_xattn, norm_mem, xa_wq, xa_wk, xa_wv, xa_wo, norm_mlp, mlp_w1, mlp_w2, norm_final):
    depth = w_in.shape[0]
    d_s5 = s5_w_glu.shape[-1]
    d_ssd = ssd_norm.shape[-1]
    d_conv = ssd_conv_w.shape[-1]
    nheads = ssd_dt_bias.shape[-1]
    rows = lambda v: v[:, None, :]
    padl = lambda v: jnp.pad(v, ((0, 0), (0, LANES - v.shape[-1])))
    bf = lambda v: v.astype(BF16)

    w_in_p = bf(jnp.pad(w_in, ((0, 0), (0, 0), (0, LANES - nheads))))
    bblk, cblk, tab = _s5_prep(s5_a_re, s5_a_im, s5_log_dt, s5_b_re, s5_b_im, s5_c_re, s5_c_im)
    dskip = s5_d.reshape(depth, -1, 1, LANES)
    a_log = padl(ssd_a_log)
    a_log_col = jnp.broadcast_to(a_log[:, :, None], (depth, LANES, LANES))
    dexp = jnp.repeat(ssd_d, SSD_HEADDIM, axis=-1)
    w_glu, w_o, wq, wo, w1, w2 = (bf(s5_w_glu), bf(w_out), bf(xa_wq), bf(xa_wo),
                                  bf(mlp_w1), bf(mlp_w2))
    k_all, v_all = _kv(mem, rows(norm_mem), bf(xa_wk), bf(xa_wv))

    for l in range(depth):
        u, ya, yb = _mix(x, rows(norm_mix), w_in_p, ssd_conv_w, rows(ssd_conv_b),
                         rows(padl(ssd_dt_bias)), rows(a_log), a_log_col, rows(dexp),
                         rows(ssd_norm), l, d_s5, d_ssd, d_conv)
        y_s5 = _s5(u, bblk, cblk, tab, dskip, w_glu, l)
        x = _attn(x, y_s5, ya, yb, w_o, rows(norm_xattn), wq, wo, k_all, v_all, l)
        x = _mlp(x, rows(norm_mlp), w1, w2, norm_final.reshape(1, -1), l, l == depth - 1)
    return x
```

```python
import functools
import math

import jax
import jax.numpy as jnp
from jax import lax
from jax.experimental import pallas as pl
from jax.experimental.pallas import tpu as pltpu

F32 = jnp.float32
BF16 = jnp.bfloat16
EPS = 1e-5

LANES = 128
SUBLANES = 8

S5_GROUP = 16
S5_STATE = 64
SSD_HEADDIM = 64
SSD_GROUPS = 4
SSD_STATE = 128
SSD_CONV = 4
SSD_CHUNK = 128
XA_HEADS = 4

S5_TILE = 512
S5_SEG = S5_TILE // SUBLANES
S5_PITCH = S5_SEG + SUBLANES
S5_ROWS = SUBLANES * S5_PITCH
S5_CB = 8
S5_LB = 4
S5_SPLIT = 2

TOK_TILE = 512
ATTN_TILE = 1024
MIX_HALF = 256
PROJ_COLS = 512
CONV_COLS = 512
VMEM_LIMIT = 56 * 1024 * 1024


def _layer_spec(arr, l):
    nd = arr.ndim - 1
    return pl.BlockSpec((None,) + arr.shape[1:], lambda *_: (l,) + (0,) * nd)


def _cparams(sem):
    return pltpu.CompilerParams(dimension_semantics=sem, vmem_limit_bytes=VMEM_LIMIT)


def _rms(x, g):
    return x * lax.rsqrt(jnp.mean(x * x, axis=-1, keepdims=True) + EPS) * g


def _dot(a, b):
    return jnp.dot(a, b, preferred_element_type=F32)


def _dot_nt(a, b):
    return lax.dot_general(a, b, (((1,), (1,)), ((), ())), preferred_element_type=F32)


def _sigmoid(x):
    return 1.0 / (1.0 + jnp.exp(-x))


def _s5_prep_kernel(arb_ref, aib_ref, ldb_ref, br_ref, bi_ref, ar_ref, ai_ref, ld_ref,
                    bbr_ref, bbi_ref, tab_ref):
    def disc(ar, ai, ld):
        dt = jnp.exp(ld)
        mag = jnp.exp(dt * ar)
        return mag * jnp.cos(dt * ai), mag * jnp.sin(dt * ai)

    ar, ai = arb_ref[...], aib_ref[...]
    abr, abi = disc(ar, ai, ldb_ref[...])
    den = ar * ar + ai * ai
    zr, zi = abr - 1.0, abi
    fr = (zr * ar + zi * ai) / den
    fi = (zi * ar - zr * ai) / den
    br, bi = br_ref[...], bi_ref[...]
    bbr_ref[...] = fr * br - fi * bi
    bbi_ref[...] = fr * bi + fi * br

    pr, pi = disc(ar_ref[...], ai_ref[...], ld_ref[...])
    tab_ref[0], tab_ref[1] = pr, pi
    for _ in range(int(math.log2(S5_SEG))):
        pr, pi = pr * pr - pi * pi, 2.0 * pr * pi
    tab_ref[2], tab_ref[3] = pr, pi
    pr, pi = pr * pr - pi * pi, 2.0 * pr * pi
    tab_ref[4], tab_ref[5] = pr, pi
    pr, pi = pr * pr - pi * pi, 2.0 * pr * pi
    tab_ref[6], tab_ref[7] = pr, pi


def _s5_prep(a_re, a_im, log_dt, b_re, b_im, c_re, c_im):
    nl, g, p = a_re.shape
    h = b_re.shape[-1]
    rep = lambda v: jnp.repeat(v, h, axis=-1)
    ldt = jnp.broadcast_to(log_dt[:, :, None], (nl, g, p))
    nlb = g * p // LANES
    flat = lambda v: v.reshape(nl, nlb, LANES)
    wide = pl.BlockSpec((None, g, p * h), lambda l: (l, 0, 0))
    narrow = pl.BlockSpec((None, nlb, LANES), lambda l: (l, 0, 0))
    bbr, bbi, tab = pl.pallas_call(
        _s5_prep_kernel,
        grid=(nl,),
        in_specs=[wide] * 5 + [narrow] * 3,
        out_specs=(wide, wide, pl.BlockSpec((None, 8, nlb, LANES), lambda l: (l, 0, 0, 0))),
        out_shape=(jax.ShapeDtypeStruct((nl, g, p * h), F32),
                   jax.ShapeDtypeStruct((nl, g, p * h), F32),
                   jax.ShapeDtypeStruct((nl, 8, nlb, LANES), F32)),
        name="s5_prep",
    )(rep(a_re), rep(a_im), rep(ldt), b_re.reshape(nl, g, p * h), b_im.reshape(nl, g, p * h),
      flat(a_re), flat(a_im), flat(ldt))
    ncb = g // 8
    eye = jnp.eye(8, dtype=BF16)
    bb = jnp.stack([bbr, bbi]).astype(BF16).reshape(2, nl, ncb, 8, p, h)
    bblk = jnp.einsum('rlcgph,gk->lcghrkp', bb, eye).reshape(nl, ncb, 8 * h, S5_SPLIT, -1)
    bblk = jnp.swapaxes(bblk, 2, 3)
    cc = jnp.stack([c_re, -c_im]).astype(BF16).reshape(2, nl, ncb, 8, h, p)
    cblk = jnp.einsum('rlcghp,gk->lcrkpgh', cc, eye).reshape(nl, ncb, S5_SPLIT, -1, 8 * h)
    tab = jnp.broadcast_to(tab[:, :, :, None, :], (nl, 8, nlb, SUBLANES, LANES))
    return bblk.astype(BF16), cblk.astype(BF16), tab


def _split3(v):
    hi = v.astype(BF16)
    r = v - hi.astype(F32)
    mid = r.astype(BF16)
    lo = (r - mid.astype(F32)).astype(BF16)
    return hi, mid, lo


def _mix_kernel(x_ref, g_ref, w_ref, cw_ref, cbias_ref, dtb_ref, alog_ref, alogc_ref, dexp_ref,
                ng_ref, u_ref, ya_ref, yb_ref, za, xa, da, zb, xb, db, hbuf, tail, state,
                *, d_s5, d_ssd, d_conv, nheads, nt):
    L = SSD_CHUNK
    P = SSD_HEADDIM
    N = SSD_STATE
    j = pl.program_id(1)

    def reset_carry():
        tail[...] = jnp.zeros_like(tail)
        state[...] = jnp.zeros_like(state)

    @pl.when(j == 0)
    def _():
        zb[...] = jnp.zeros_like(zb)
        xb[...] = jnp.zeros_like(xb)
        db[...] = jnp.zeros_like(db)
        reset_carry()

    def project_pieces(rows, z_s, x_s, d_s):
        hbuf[...] = _rms(x_ref[rows, :], g_ref[...]).astype(BF16)

        def u_store(c0):
            def f(r):
                for k in range(r.shape[1] // LANES):
                    u_ref[c0 // LANES + k, rows, :] = r[:, k * LANES:(k + 1) * LANES]
            return f

        def col_store(ref, c0):
            def f(r):
                ref[:, c0:c0 + r.shape[1]] = r
            return f

        def piece(w0, width, store):
            return lambda: store(_dot(hbuf[...], w_ref[:, w0:w0 + width]))

        pieces = [piece(c0, PROJ_COLS, u_store(c0)) for c0 in range(0, d_s5, PROJ_COLS)]
        o = d_s5
        pieces += [piece(o + c0, PROJ_COLS, col_store(z_s, c0)) for c0 in range(0, d_ssd, PROJ_COLS)]
        o += d_ssd
        pieces += [piece(o + c0, PROJ_COLS, col_store(x_s, c0)) for c0 in range(0, d_conv, PROJ_COLS)]
        o += d_conv
        pieces.append(piece(o, LANES, col_store(d_s, 0)))
        return pieces

    def emitter(pieces, nslots):
        done = [0, 0]

        def issue_upto(n):
            while done[1] < min(n, len(pieces)):
                pieces[done[1]]()
                done[1] += 1

        def emit():
            done[0] += 1
            issue_upto(-(-done[0] * len(pieces) // nslots))

        return emit, lambda: issue_upto(len(pieces))

    ri = lax.broadcasted_iota(jnp.int32, (L, L), 0)
    ci = lax.broadcasted_iota(jnp.int32, (L, L), 1)
    causal = ri >= ci
    ltri = causal.astype(BF16)
    utri = (ri <= ci).astype(BF16)
    low = lax.broadcasted_iota(jnp.int32, (1, 2 * P), 1) < P

    def ssd_chunk(z, x_s, rows, dt_raw, emit):
        blocks = []
        for c0 in range(0, d_conv, CONV_COLS):
            cols = slice(c0, c0 + CONV_COLS)
            x = x_s[rows, cols]
            xfull = jnp.concatenate([tail[:, cols], x], axis=0)
            tail[:, cols] = x[L - SUBLANES:, :]
            conv = cbias_ref[:, cols] + cw_ref[SSD_CONV - 1:SSD_CONV, cols] * x
            for k in range(SSD_CONV - 1):
                sh = SSD_CONV - 1 - k
                conv = conv + cw_ref[k:k + 1, cols] * pltpu.roll(xfull, sh, 0)[SUBLANES:, :]
            blocks.append(conv * _sigmoid(conv))
            emit()
        act = jnp.concatenate(blocks, axis=1)
        xs = act[:, :d_ssd]
        bm = act[:, d_ssd:d_ssd + SSD_GROUPS * N]
        cm = act[:, d_ssd + SSD_GROUPS * N:]

        t = dt_raw + dtb_ref[...]
        dt = jnp.maximum(t, 0.0) + jnp.log(1.0 + jnp.exp(-jnp.abs(t)))
        dt_t = dt.T
        a = dt * (-jnp.exp(alog_ref[...]))
        a_t = dt_t * (-jnp.exp(alogc_ref[...]))
        acum = sum(_dot(ltri, part) for part in _split3(a))
        acum_t = sum(_dot(part, utri) for part in _split3(a_t))
        wdec_t = dt_t * jnp.exp(acum_t[:, L - 1:L] - acum_t)
        emit()

        ys = []
        for q in range(nheads // 2):
            g = (2 * q * SSD_GROUPS) // nheads
            if (2 * q * SSD_GROUPS) % nheads == 0:
                bg = bm[:, g * N:(g + 1) * N]
                cg = cm[:, g * N:(g + 1) * N]
                cb = _dot_nt(cg.astype(BF16), bg.astype(BF16))
                bg_t = bg.T
            xs_q = xs[:, q * 2 * P:(q + 1) * 2 * P].astype(BF16)
            st = state[q]
            rhs = jnp.concatenate([xs_q, st.astype(BF16)], axis=0)
            yq, upd, el = [], [], []
            for par in range(2):
                r = 2 * q + par
                col = jnp.broadcast_to(acum[:, r:r + 1], (L, L))
                decay = jnp.exp(jnp.where(causal, col - acum_t[r:r + 1, :], -1e30))
                m = cb * decay * dt_t[r:r + 1, :]
                e_col = jnp.exp(col)
                lhs = jnp.concatenate([m, cg * e_col], axis=1).astype(BF16)
                yq.append(_dot(lhs, rhs))
                upd.append(_dot((bg_t * wdec_t[r:r + 1, :]).astype(BF16), xs_q))
                el.append(e_col[L - 1:L, :])
            ys.append(jnp.where(low, yq[0], yq[1]))
            state[q] = st * jnp.where(low, el[0], el[1]) + jnp.where(low, upd[0], upd[1])
            emit()
        y = jnp.concatenate(ys, axis=1) + dexp_ref[...] * xs
        return _rms(y * (z * _sigmoid(z)), ng_ref[...]).astype(BF16)

    def phase(proj_rows, p_scratch, s_scratch, y_ref):
        slots_per_chunk = d_conv // CONV_COLS + 1 + nheads // 2
        emit, flush = emitter(project_pieces(proj_rows, *p_scratch),
                              (MIX_HALF // L) * slots_per_chunk)
        z_s, x_s, d_s = s_scratch
        for c in range(MIX_HALF // L):
            r = slice(c * L, (c + 1) * L)
            y_ref[r, :] = ssd_chunk(z_s[r, :], x_s, r, d_s[r, :], emit)
        flush()

    phase(slice(0, MIX_HALF), (za, xa, da), (zb, xb, db), yb_ref)

    @pl.when(j == 0)
    def _():
        reset_carry()

    @pl.when(j < nt)
    def _():
        phase(slice(MIX_HALF, 2 * MIX_HALF), (zb, xb, db), (za, xa, da), ya_ref)


def _mix(x, g, w, conv_w, conv_b, dt_bias, a_log, a_log_col, dexp, norm_g, l, d_s5, d_ssd, d_conv):
    b, s, d = x.shape
    tm = 2 * MIX_HALF
    nt = s // tm
    ncb = d_s5 // LANES
    nheads = d_ssd // SSD_HEADDIM
    params = (g, w, conv_w, conv_b, dt_bias, a_log, a_log_col, dexp, norm_g)
    cur = lambda i, j: jnp.minimum(j, nt - 1)
    prev = lambda i, j: jnp.maximum(j - 1, 0)
    half = lambda n: pltpu.VMEM((MIX_HALF, n), F32)
    yshape = jax.ShapeDtypeStruct((b, nt, MIX_HALF, d_ssd), BF16)
    return pl.pallas_call(
        functools.partial(_mix_kernel, d_s5=d_s5, d_ssd=d_ssd, d_conv=d_conv, nheads=nheads, nt=nt),
        grid=(b, nt + 1),
        in_specs=[pl.BlockSpec((None, tm, d), lambda i, j: (i, cur(i, j), 0))]
                 + [_layer_spec(p, l) for p in params],
        out_specs=(pl.BlockSpec((None, ncb, tm, LANES), lambda i, j: (i, 0, cur(i, j), 0)),
                   pl.BlockSpec((None, None, MIX_HALF, d_ssd), lambda i, j: (i, cur(i, j), 0, 0)),
                   pl.BlockSpec((None, None, MIX_HALF, d_ssd), lambda i, j: (i, prev(i, j), 0, 0))),
        out_shape=(jax.ShapeDtypeStruct((b, ncb, s, LANES), F32), yshape, yshape),
        scratch_shapes=[half(d_ssd), half(d_conv), half(LANES),
                        half(d_ssd), half(d_conv), half(LANES),
                        pltpu.VMEM((MIX_HALF, d), BF16),
                        pltpu.VMEM((SUBLANES, d_conv), F32),
                        pltpu.VMEM((nheads // 2, SSD_STATE, 2 * SSD_HEADDIM), F32)],
        compiler_params=_cparams(("arbitrary", "arbitrary")),
        name="inproj_ssd",
    )(x, *params)


def _s5_kernel(u_ref, bblk_ref, cblk_ref, tab_ref, dskip_ref, wglu_ref, y_ref,
               buf0, buf1, buf2, upad, ubf, yacc, ypad, carry):
    bufs = (buf0, buf1, buf2)
    nsl = 2 * S5_LB
    part = S5_SEG // S5_SPLIT
    psl = nsl // S5_SPLIT

    @pl.when(pl.program_id(1) == 0)
    def _():
        carry[...] = jnp.zeros_like(carry)

    row = lax.broadcasted_iota(jnp.int32, (SUBLANES, LANES), 0)

    def shift_down(v, k):
        return jnp.where(row >= k, pltpu.roll(v, k, 0), 0.0)

    def cmul(ar, ai, xr, xi):
        return ar * xr - ai * xi, ar * xi + ai * xr

    def prep_u(c):
        for j in range(SUBLANES):
            upad[j * S5_PITCH:j * S5_PITCH + S5_SEG, :] = u_ref[c, j * S5_SEG:(j + 1) * S5_SEG, :]
        ubf[...] = jnp.concatenate(
            [upad[pl.ds(i, SUBLANES, stride=S5_PITCH), :] for i in range(S5_SEG)],
            axis=0).astype(BF16)

    def drive_part(c, it, dst):
        res = _dot(ubf[...], bblk_ref[c, it])
        for k in range(psl):
            dst[psl * it + k] = res[:, k * LANES:(k + 1) * LANES]

    def readout_part(c, it, src):
        lhs = jnp.concatenate([src[psl * it + k] for k in range(psl)], axis=1)
        yacc[...] += _dot(lhs.astype(BF16), cblk_ref[c, it])

    def unpermute(c):
        for i in range(S5_SEG):
            ypad[c, pl.ds(i, SUBLANES, stride=S5_PITCH), :] = yacc[i * SUBLANES:(i + 1) * SUBLANES, :]

    def scan_steps(cur, base, st, ar, ai, store):
        for s in range(part):
            r = pl.ds(pl.multiple_of((base + s) * SUBLANES, SUBLANES), SUBLANES)
            out = []
            for k in range(S5_LB):
                pr, pi = cmul(ar[k], ai[k], st[2 * k], st[2 * k + 1])
                nr, ni = pr + cur[k, r, :], pi + cur[S5_LB + k, r, :]
                if store:
                    cur[k, r, :] = nr
                    cur[S5_LB + k, r, :] = ni
                out += [nr, ni]
            st = tuple(out)
        return st

    prep_u(0)
    for it in range(S5_SPLIT):
        drive_part(0, it, bufs[0])

    zero = jnp.zeros((SUBLANES, LANES), F32)
    for cb in range(S5_CB):
        cur, nxt, prv = bufs[cb % 3], bufs[(cb + 1) % 3], bufs[(cb + 2) % 3]
        lbs = [cb * S5_LB + k for k in range(S5_LB)]
        ar = [tab_ref[0, lb] for lb in lbs]
        ai = [tab_ref[1, lb] for lb in lbs]
        has_next, has_prev = cb + 1 < S5_CB, cb > 0

        if has_next:
            prep_u(cb + 1)

        def pass1(it, st, cur=cur, nxt=nxt, ar=ar, ai=ai, cb=cb, has_next=has_next):
            if has_next:
                drive_part(cb + 1, it, nxt)
            return scan_steps(cur, it * part, st, ar, ai, False)

        ends = (zero,) * nsl
        for it in range(S5_SPLIT):
            ends = pass1(it, ends)

        init = []
        for k in range(S5_LB):
            lb = lbs[k]
            er, ei = ends[2 * k], ends[2 * k + 1]
            xr = jnp.where(row == 0, carry[0, lb], shift_down(er, 1))
            xi = jnp.where(row == 0, carry[1, lb], shift_down(ei, 1))
            for lvl, sh in ((2, 1), (4, 2), (6, 4)):
                pr, pi = cmul(tab_ref[lvl, lb], tab_ref[lvl + 1, lb],
                              shift_down(xr, sh), shift_down(xi, sh))
                xr, xi = xr + pr, xi + pi
            pr, pi = cmul(tab_ref[2, lb], tab_ref[3, lb], xr, xi)
            nr, ni = pr + er, pi + ei
            carry[0, lb] = jnp.broadcast_to(nr[SUBLANES - 1:SUBLANES, :], (SUBLANES, LANES))
            carry[1, lb] = jnp.broadcast_to(ni[SUBLANES - 1:SUBLANES, :], (SUBLANES, LANES))
            init += [xr, xi]

        if has_prev:
            yacc[...] = jnp.zeros_like(yacc)

        def pass2(it, st, cur=cur, prv=prv, ar=ar, ai=ai, cb=cb, has_prev=has_prev):
            if has_prev:
                readout_part(cb - 1, it, prv)
            return scan_steps(cur, it * part, st, ar, ai, True)

        st = tuple(init)
        for it in range(S5_SPLIT):
            st = pass2(it, st)
        if has_prev:
            unpermute(cb - 1)

    last = S5_CB - 1
    yacc[...] = jnp.zeros_like(yacc)
    for it in range(S5_SPLIT):
        readout_part(last, it, bufs[last % 3])
    unpermute(last)

    y = jnp.concatenate(
        [jnp.concatenate([ypad[c, j * S5_PITCH:j * S5_PITCH + S5_SEG, :]
                          for j in range(SUBLANES)], axis=0) + dskip_ref[c] * u_ref[c]
         for c in range(S5_CB)], axis=1)
    y = jax.nn.gelu(y, approximate=True)
    y_ref[...] = (y * _sigmoid(_dot(y.astype(BF16), wglu_ref[...]))).astype(BF16)


def _s5(u, bblk, cblk, tab, dskip, wglu, l):
    b, ncb, s, _ = u.shape
    d = ncb * LANES
    nlb = tab.shape[2]
    slab = pltpu.VMEM((2 * S5_LB, S5_TILE, LANES), F32)
    return pl.pallas_call(
        _s5_kernel,
        grid=(b, s // S5_TILE),
        in_specs=[pl.BlockSpec((None, ncb, S5_TILE, LANES), lambda i, j: (i, 0, j, 0)),
                  _layer_spec(bblk, l), _layer_spec(cblk, l), _layer_spec(tab, l),
                  _layer_spec(dskip, l), _layer_spec(wglu, l)],
        out_specs=pl.BlockSpec((None, S5_TILE, d), lambda i, j: (i, j, 0)),
        out_shape=jax.ShapeDtypeStruct((b, s, d), BF16),
        scratch_shapes=[slab, slab, slab,
                        pltpu.VMEM((S5_ROWS, LANES), F32),
                        pltpu.VMEM((S5_TILE, LANES), BF16),
                        pltpu.VMEM((S5_TILE, LANES), F32),
                        pltpu.VMEM((ncb, S5_ROWS, LANES), F32),
                        pltpu.VMEM((2, nlb, SUBLANES, LANES), F32)],
        compiler_params=_cparams(("arbitrary", "arbitrary")),
        name="s5_mixer",
    )(u, bblk, cblk, tab, dskip, wglu)


def _kv_kernel(mem_ref, g_ref, wk_ref, wv_ref, k_ref, v_ref):
    m = _rms(mem_ref[...], g_ref[...]).astype(BF16)
    k_ref[...] = _dot(m, wk_ref[...]).astype(BF16)
    v_ref[...] = _dot(m, wv_ref[...]).astype(BF16)


def _kv(mem, norm_mem, wk, wv):
    b, m, d = mem.shape
    nl = wk.shape[0]
    out = jax.ShapeDtypeStruct((nl, b, m, d), BF16)
    return pl.pallas_call(
        _kv_kernel,
        grid=(nl, b),
        in_specs=[pl.BlockSpec((None, m, d), lambda l, i: (i, 0, 0)),
                  pl.BlockSpec((None, 1, d), lambda l, i: (l, 0, 0)),
                  pl.BlockSpec((None, d, d), lambda l, i: (l, 0, 0)),
                  pl.BlockSpec((None, d, d), lambda l, i: (l, 0, 0))],
        out_specs=(pl.BlockSpec((None, None, m, d), lambda l, i: (l, i, 0, 0)),
                   pl.BlockSpec((None, None, m, d), lambda l, i: (l, i, 0, 0))),
        out_shape=(out, out),
        compiler_params=_cparams(("arbitrary", "arbitrary")),
        name="mem_kv",
    )(mem, norm_mem, wk, wv)


def _attn_kernel(x_ref, y1_ref, y2a_ref, y2b_ref, wout_ref, g_ref, wq_ref, wo_ref, k_ref, v_ref,
                 o_ref):
    d1 = y1_ref.shape[-1]
    y2 = jnp.concatenate([ref[t] for t in range(y2a_ref.shape[0]) for ref in (y2a_ref, y2b_ref)],
                         axis=0)
    x = x_ref[...] + _dot(y1_ref[...], wout_ref[:d1, :]) + _dot(y2, wout_ref[d1:, :])
    d = x.shape[-1]
    hd = d // XA_HEADS
    h = _rms(x, g_ref[...]).astype(BF16)
    q = (_dot(h, wq_ref[...]) * (hd ** -0.5)).astype(BF16)
    outs = []
    for a in range(XA_HEADS):
        sl = slice(a * hd, (a + 1) * hd)
        s = _dot_nt(q[:, sl], k_ref[:, sl])
        e = jnp.exp(s - jnp.max(s, axis=-1, keepdims=True))
        p = e / jnp.sum(e, axis=-1, keepdims=True)
        outs.append(_dot(p.astype(BF16), v_ref[:, sl]))
    o = jnp.concatenate(outs, axis=1).astype(BF16)
    o_ref[...] = x + _dot(o, wo_ref[...])


def _attn(x, y1, y2a, y2b, wout, g, wq, wo, k, v, l):
    b, s, d = x.shape
    m = k.shape[2]
    tm = ATTN_TILE
    halfspec = pl.BlockSpec((None, tm // (2 * MIX_HALF), MIX_HALF, y2a.shape[-1]),
                            lambda i, j: (i, j, 0, 0))
    tile = lambda n: pl.BlockSpec((None, tm, n), lambda i, j: (i, j, 0))
    kv = pl.BlockSpec((None, None, m, d), lambda i, j: (l, i, 0, 0))
    return pl.pallas_call(
        _attn_kernel,
        grid=(b, s // tm),
        in_specs=[tile(d), tile(y1.shape[-1]), halfspec, halfspec, _layer_spec(wout, l),
                  _layer_spec(g, l), _layer_spec(wq, l), _layer_spec(wo, l), kv, kv],
        out_specs=tile(d),
        out_shape=jax.ShapeDtypeStruct(x.shape, F32),
        compiler_params=_cparams(("arbitrary", "arbitrary")),
        name="outproj_xattn",
    )(x, y1, y2a, y2b, wout, g, wq, wo, k, v)


def _mlp_kernel(x_ref, g_ref, w1_ref, w2_ref, gf_ref, o_ref, *, final_norm):
    x = x_ref[...]
    d = x.shape[-1]
    h = _rms(x, g_ref[...]).astype(BF16)
    acc = x
    for c in range(w1_ref.shape[1] // d):
        a = jnp.maximum(_dot(h, w1_ref[:, c * d:(c + 1) * d]), 0.0)
        acc = acc + _dot((a * a).astype(BF16), w2_ref[c * d:(c + 1) * d, :])
    o_ref[...] = _rms(acc, gf_ref[...]) if final_norm else acc


def _mlp(x, g, w1, w2, gf, l, final_norm):
    b, s, d = x.shape
    tm = TOK_TILE
    tile = pl.BlockSpec((None, tm, d), lambda i, j: (i, j, 0))
    vec = pl.BlockSpec((1, d), lambda i, j: (0, 0))
    return pl.pallas_call(
        functools.partial(_mlp_kernel, final_norm=final_norm),
        grid=(b, s // tm),
        in_specs=[tile, _layer_spec(g, l), _layer_spec(w1, l), _layer_spec(w2, l), vec],
        out_specs=tile,
        out_shape=jax.ShapeDtypeStruct(x.shape, F32),
        compiler_params=_cparams(("arbitrary", "arbitrary")),
        name="mlp",
    )(x, g, w1, w2, gf)


def kernel(x, mem, norm_mix, w_in, s5_a_re, s5_a_im, s5_log_dt, s5_b_re, s5_b_im, s5_c_re, s5_c_im, s5_d, s5_w_glu, ssd_conv_w, ssd_conv_b, ssd_dt_bias, ssd_a_log, ssd_d, ssd_norm, w_out, norm_xattn, norm_mem, xa_wq, xa_wk, xa_wv, xa_wo, norm_mlp, mlp_w1, mlp_w2, norm_final):
    depth = w_in.shape[0]
    d_s5 = s5_w_glu.shape[-1]
    d_ssd = ssd_norm.shape[-1]
    d_conv = ssd_conv_w.shape[-1]
    nheads = ssd_dt_bias.shape[-1]
    rows = lambda v: v[:, None, :]
    padl = lambda v: jnp.pad(v, ((0, 0), (0, LANES - v.shape[-1])))
    bf = lambda v: v.astype(BF16)

    w_in_p = bf(jnp.pad(w_in, ((0, 0), (0, 0), (0, LANES - nheads))))
    bblk, cblk, tab = _s5_prep(s5_a_re, s5_a_im, s5_log_dt, s5_b_re, s5_b_im, s5_c_re, s5_c_im)
    dskip = s5_d.reshape(depth, -1, 1, LANES)
    a_log = padl(ssd_a_log)
    a_log_col = jnp.broadcast_to(a_log[:, :, None], (depth, LANES, LANES))
    dexp = jnp.repeat(ssd_d, SSD_HEADDIM, axis=-1)
    w_glu, w_o, wq, wo, w1, w2 = (bf(s5_w_glu), bf(w_out), bf(xa_wq), bf(xa_wo),
                                  bf(mlp_w1), bf(mlp_w2))
    k_all, v_all = _kv(mem, rows(norm_mem), bf(xa_wk), bf(xa_wv))

    for l in range(depth):
        u, ya, yb = _mix(x, rows(norm_mix), w_in_p, ssd_conv_w, rows(ssd_conv_b),
                         rows(padl(ssd_dt_bias)), rows(a_log), a_log_col, rows(dexp),
                         rows(ssd_norm), l, d_s5, d_ssd, d_conv)
        y_s5 = _s5(u, bblk, cblk, tab, dskip, w_glu, l)
        x = _attn(x, y_s5, ya, yb, w_o, rows(norm_xattn), wq, wo, k_all, v_all, l)
        x = _mlp(x, rows(norm_mlp), w1, w2, norm_final.reshape(1, -1), l, l == depth - 1)
    return x
```
